```python
import math
import jax, jax.numpy as jnp
from jax import lax
import numpy as np

D_MODEL = 1024
BATCH = 4
SEQ = 8192
DEPTH = 2

DIFF_HEADS = 4
DIFF_HEAD_DIM = 64
DIFF_WIDTH = DIFF_HEADS * 2 * DIFF_HEAD_DIM
SB_HEADS = 8
SB_HEAD_DIM = 64
SB_WIDTH = SB_HEADS * SB_HEAD_DIM
IN_COLS = 3 * DIFF_WIDTH + 3 * SB_WIDTH + 2 * D_MODEL
FFN_HIDDEN = -(-(8 * D_MODEL) // (3 * 256)) * 256
Q_BLOCK = 128
ROPE_THETA = 10000.0
NORM_EPS = 1e-6

kernel_name = "hybrid_diffattn_stickbreaking_gated"


def rms_norm(x, g):
    xf = x.astype(jnp.float32)
    y = xf * lax.rsqrt(jnp.mean(xf * xf, axis=-1, keepdims=True) + NORM_EPS)
    return (y * g.astype(jnp.float32)).astype(x.dtype)


def rope_tables(seq, dim):
    pos = jnp.arange(seq, dtype=jnp.float32)
    inv = ROPE_THETA ** (-jnp.arange(0, dim, 2, dtype=jnp.float32) / dim)
    ang = pos[:, None] * inv[None, :]
    return jnp.cos(ang), jnp.sin(ang)


def apply_rope(t, cos, sin):
    c = cos[None, :, None, :].astype(t.dtype)
    s = sin[None, :, None, :].astype(t.dtype)
    t1, t2 = jnp.split(t, 2, axis=-1)
    return jnp.concatenate([t1 * c - t2 * s, t2 * c + t1 * s], axis=-1)


def to_blocks(t):
    b, h, s, d = t.shape
    return t.reshape(b, h, s // Q_BLOCK, Q_BLOCK, d).transpose(2, 0, 1, 3, 4)


def from_blocks(o):
    nb, b, h, q, dv = o.shape
    return o.transpose(1, 0, 3, 2, 4).reshape(b, nb * q, h, dv)


def diff_attention(q1, q2, k1, k2, v, lam):
    s_len = k1.shape[2]
    kpos = jnp.arange(s_len)
    qpos_blocks = kpos.reshape(-1, Q_BLOCK)
    scale = DIFF_HEAD_DIM ** -0.5

    def block(args):
        qb1, qb2, qpos = args
        mask = kpos[None, :] <= qpos[:, None]

        def probs(qb, k):
            sc = jnp.einsum('bhqd,bhkd->bhqk', qb, k).astype(jnp.float32) * scale
            sc = jnp.where(mask, sc, -jnp.inf)
            return jax.nn.softmax(sc, axis=-1)

        w = probs(qb1, k1) - lam * probs(qb2, k2)
        return jnp.einsum('bhqk,bhkd->bhqd', w.astype(v.dtype), v)

    o = lax.map(block, (to_blocks(q1), to_blocks(q2), qpos_blocks))
    return from_blocks(o)


def stick_breaking_attention(q, k, v):
    s_len = k.shape[2]
    kpos = jnp.arange(s_len)
    qpos_blocks = kpos.reshape(-1, Q_BLOCK)
    scale = SB_HEAD_DIM ** -0.5

    def block(args):
        qb, qpos = args
        mask = kpos[None, :] < qpos[:, None]
        z = jnp.einsum('bhqd,bhkd->bhqk', qb, k).astype(jnp.float32) * scale
        log_beta = jax.nn.log_sigmoid(z)
        log_rem = jnp.where(mask, jax.nn.log_sigmoid(-z), 0.0)
        key_axis = log_rem.ndim - 1
        after = lax.cumsum(log_rem, axis=key_axis, reverse=True) - log_rem
        a = jnp.where(mask, jnp.exp(log_beta + after), 0.0)
        return jnp.einsum('bhqk,bhkd->bhqd', a.astype(v.dtype), v)

    o = lax.map(block, (to_blocks(q), qpos_blocks))
    return from_blocks(o)


def hybrid_layer(x, cos, sin, layer, g_attn, w_in, b_gate, lam_p, subln, w_o_diff, w_o_sb,
                 w_out, g_ffn, w_ffn_in, w_ffn_out):
    b, s, _ = x.shape
    h = rms_norm(x, g_attn)
    proj = h @ w_in
    offs = np.cumsum([DIFF_WIDTH, DIFF_WIDTH, DIFF_WIDTH, SB_WIDTH, SB_WIDTH, SB_WIDTH]).tolist()
    dq, dk, dv, sq, sk, sv, gates = jnp.split(proj, offs, axis=-1)

    dq = apply_rope(dq.reshape(b, s, 2 * DIFF_HEADS, DIFF_HEAD_DIM), cos, sin)
    dk = apply_rope(dk.reshape(b, s, 2 * DIFF_HEADS, DIFF_HEAD_DIM), cos, sin)
    dq = dq.reshape(b, s, DIFF_HEADS, 2, DIFF_HEAD_DIM).transpose(3, 0, 2, 1, 4)
    dk = dk.reshape(b, s, DIFF_HEADS, 2, DIFF_HEAD_DIM).transpose(3, 0, 2, 1, 4)
    dv = dv.reshape(b, s, DIFF_HEADS, 2 * DIFF_HEAD_DIM).transpose(0, 2, 1, 3)
    lam_init = 0.8 - 0.6 * math.exp(-0.3 * layer)
    lp = lam_p.astype(jnp.float32)
    lam = jnp.exp(jnp.sum(lp[0] * lp[1])) - jnp.exp(jnp.sum(lp[2] * lp[3])) + lam_init
    o_diff = diff_attention(dq[0], dq[1], dk[0], dk[1], dv, lam)
    o_diff = (rms_norm(o_diff, subln) * (1.0 - lam_init)).reshape(b, s, DIFF_WIDTH)

    sq = sq.reshape(b, s, SB_HEADS, SB_HEAD_DIM).transpose(0, 2, 1, 3)
    sk = sk.reshape(b, s, SB_HEADS, SB_HEAD_DIM).transpose(0, 2, 1, 3)
    sv = sv.reshape(b, s, SB_HEADS, SB_HEAD_DIM).transpose(0, 2, 1, 3)
    o_sb = stick_breaking_attention(sq, sk, sv).reshape(b, s, SB_WIDTH)

    g_diff, g_sb = jnp.split(jax.nn.sigmoid(gates + b_gate), 2, axis=-1)
    merged = g_diff * (o_diff @ w_o_diff) + g_sb * (o_sb @ w_o_sb)
    x = x + merged @ w_out

    h2 = rms_norm(x, g_ffn)
    gate, up = jnp.split(h2 @ w_ffn_in, 2, axis=-1)
    return x + (jax.nn.silu(gate) * up) @ w_ffn_out


def setup_inputs(seed: int = 0) -> dict:
    key = jax.random.key(seed)
    ks = jax.random.split(key, 14)
    f32 = jnp.float32
    nrm = lambda k, shape, scale: jax.random.normal(k, shape, f32) * scale
    return {
        "x": nrm(ks[0], (BATCH, SEQ, D_MODEL), 1.0),
        "norm_attn": 1.0 + nrm(ks[1], (DEPTH, D_MODEL), 0.02),
        "w_in": nrm(ks[2], (DEPTH, D_MODEL, IN_COLS), D_MODEL ** -0.5),
        "b_gate": nrm(ks[3], (DEPTH, 2 * D_MODEL), 0.02),
        "diff_lambda": nrm(ks[4], (DEPTH, 4, DIFF_HEAD_DIM), 0.1),
        "diff_subln": 1.0 + nrm(ks[5], (DEPTH, 2 * DIFF_HEAD_DIM), 0.02),
        "w_o_diff": nrm(ks[6], (DEPTH, DIFF_WIDTH, D_MODEL), DIFF_WIDTH ** -0.5),
        "w_o_sb": nrm(ks[7], (DEPTH, SB_WIDTH, D_MODEL), SB_WIDTH ** -0.5),
        "w_out": nrm(ks[8], (DEPTH, D_MODEL, D_MODEL), D_MODEL ** -0.5),
        "norm_ffn": 1.0 + nrm(ks[9], (DEPTH, D_MODEL), 0.02),
        "w_ffn_in": nrm(ks[10], (DEPTH, D_MODEL, 2 * FFN_HIDDEN), D_MODEL ** -0.5),
        "w_ffn_out": nrm(ks[11], (DEPTH, FFN_HIDDEN, D_MODEL), FFN_HIDDEN ** -0.5),
        "norm_final": 1.0 + nrm(ks[12], (D_MODEL,), 0.02),
    }


def reference(x, norm_attn, w_in, b_gate, diff_lambda, diff_subln, w_o_diff, w_o_sb, w_out,
              norm_ffn, w_ffn_in, w_ffn_out, norm_final):
    cos, sin = rope_tables(x.shape[1], DIFF_HEAD_DIM)
    for layer in range(DEPTH):
        x = hybrid_layer(x, cos, sin, layer, norm_attn[layer], w_in[layer], b_gate[layer],
                         diff_lambda[layer], diff_subln[layer], w_o_diff[layer], w_o_sb[layer],
                         w_out[layer], norm_ffn[layer], w_ffn_in[layer], w_ffn_out[layer])
    return rms_norm(x, norm_final)
```

```python
import functools
import math

import jax
import jax.numpy as jnp
from jax import lax
from jax.experimental import pallas as pl
from jax.experimental.pallas import tpu as pltpu

D_MODEL = 1024
DEPTH = 2
HEAD_DIM = 64
LANES = 128
DIFF_HEADS = 4
DIFF_WIDTH = DIFF_HEADS * 2 * HEAD_DIM
SB_HEADS = 8
SB_WIDTH = SB_HEADS * HEAD_DIM
SB_PAIRS = SB_WIDTH // LANES
IN_COLS = 3 * DIFF_WIDTH + 3 * SB_WIDTH + 2 * D_MODEL
GATE_COL0 = 3 * DIFF_WIDTH + 3 * SB_WIDTH
FFN_HIDDEN = 2816
ROPE_THETA = 10000.0
NORM_EPS = 1e-6
NEG_BIG = -1e30

VMEM_LIMIT = 48 * 1024 * 1024

IN_TM, IN_TN = 1024, 512
DIFF_TQ, DIFF_TK = 512, 512
SB_TQ, SB_TK = 512, 256
MERGE_TM = 512
FFN_TM, FFN_TH = 1024, 1408

BF16 = jnp.bfloat16
F32 = jnp.float32


def _nt_dot(a, b):
    return lax.dot_general(a, b, (((1,), (1,)), ((), ())), preferred_element_type=F32)


def _in_proj_kernel(x_ref, g_ref, w_ref, bg_ref, cos_ref, sin_ref, o_ref, h_ref):
    j = pl.program_id(1)

    @pl.when(j == 0)
    def _():
        x = x_ref[...]
        ms = jnp.mean(x * x, axis=-1, keepdims=True)
        h_ref[...] = (x * lax.rsqrt(ms + NORM_EPS) * g_ref[...]).astype(BF16)

    acc = jnp.dot(h_ref[...], w_ref[...], preferred_element_type=F32)

    n_diff_q = DIFF_WIDTH // IN_TN
    n_rope = 2 * DIFF_WIDTH // IN_TN
    sb_q0 = 3 * DIFF_WIDTH // IN_TN
    sb_q1 = (3 * DIFF_WIDTH + SB_WIDTH) // IN_TN
    gate0 = GATE_COL0 // IN_TN
    scale = HEAD_DIM ** -0.5

    @pl.when(j < n_rope)
    def _():
        cos = cos_ref[...]
        sin = sin_ref[...]
        lane = lax.broadcasted_iota(jnp.int32, cos.shape, 1)
        first_half = (lane % HEAD_DIM) < (HEAD_DIM // 2)
        sc = jnp.where(j < n_diff_q, scale, 1.0).astype(F32)
        for c in range(IN_TN // LANES):
            t = acc[:, c * LANES:(c + 1) * LANES]
            partner = jnp.where(first_half,
                                pltpu.roll(t, LANES - HEAD_DIM // 2, 1),
                                pltpu.roll(t, HEAD_DIM // 2, 1))
            r = (t * cos + partner * sin) * sc
            o_ref[:, c * LANES:(c + 1) * LANES] = r.astype(BF16)

    @pl.when((j >= n_rope) & (j < gate0))
    def _():
        sc = jnp.where((j >= sb_q0) & (j < sb_q1), scale, 1.0).astype(F32)
        o_ref[...] = (acc * sc).astype(BF16)

    @pl.when(j >= gate0)
    def _():
        o_ref[...] = jax.nn.sigmoid(acc + bg_ref[...]).astype(BF16)


def _in_proj(x2d, g, w, bg_full, cos_t, sin_t, seq):
    t_rows = x2d.shape[0]
    n_i, n_j = t_rows // IN_TM, IN_COLS // IN_TN
    pos_tiles = seq // IN_TM
    return pl.pallas_call(
        _in_proj_kernel,
        grid=(n_i, n_j),
        in_specs=[
            pl.BlockSpec((IN_TM, D_MODEL), lambda i, j: (i, 0)),
            pl.BlockSpec((1, D_MODEL), lambda i, j: (0, 0)),
            pl.BlockSpec((D_MODEL, IN_TN), lambda i, j: (0, j)),
            pl.BlockSpec((1, IN_TN), lambda i, j: (0, j)),
            pl.BlockSpec((IN_TM, LANES), lambda i, j: (i % pos_tiles, 0)),
            pl.BlockSpec((IN_TM, LANES), lambda i, j: (i % pos_tiles, 0)),
        ],
        out_specs=pl.BlockSpec((IN_TM, IN_TN), lambda i, j: (i, j)),
        out_shape=jax.ShapeDtypeStruct((t_rows, IN_COLS), BF16),
        scratch_shapes=[pltpu.VMEM((IN_TM, D_MODEL), BF16)],
        compiler_params=pltpu.CompilerParams(
            dimension_semantics=("arbitrary", "arbitrary"), vmem_limit_bytes=VMEM_LIMIT),
        name="in_proj",
    )(x2d, g, w, bg_full, cos_t, sin_t)


def _diff_attn_kernel(q_ref, k_ref, v_ref, lam_ref, subln_ref, o_ref, *, lam_init):
    tq, tk = DIFF_TQ, DIFF_TK
    qi = pl.program_id(2)
    q = q_ref[...]
    lane = lax.broadcasted_iota(jnp.int32, q.shape, 1)
    zero = jnp.zeros_like(q)
    q1 = jnp.where(lane < HEAD_DIM, q, zero)
    q2 = jnp.where(lane >= HEAD_DIM, q, zero)

    def softmax_step(qm, k, v, m, l, a, mask):
        s = _nt_dot(qm, k)
        if mask is not None:
            s = jnp.where(mask, s, NEG_BIG)
        m_new = jnp.maximum(m, jnp.max(s, axis=1, keepdims=True))
        alpha = jnp.exp(m - m_new)
        p = jnp.exp(s - m_new)
        l = alpha * l + jnp.sum(p, axis=1, keepdims=True)
        a = alpha * a + jnp.dot(p.astype(BF16), v, preferred_element_type=F32)
        return m_new, l, a

    def tile(kb, carry, mask):
        m1, l1, a1, m2, l2, a2 = carry
        start = pl.multiple_of(kb * tk, tk)
        k = k_ref[pl.ds(start, tk), :]
        v = v_ref[pl.ds(start, tk), :]
        m1, l1, a1 = softmax_step(q1, k, v, m1, l1, a1, mask)
        m2, l2, a2 = softmax_step(q2, k, v, m2, l2, a2, mask)
        return m1, l1, a1, m2, l2, a2

    col0 = jnp.full((tq, 1), NEG_BIG, F32)
    zcol = jnp.zeros((tq, 1), F32)
    zacc = jnp.zeros((tq, LANES), F32)
    carry = (col0, zcol, zacc, col0, zcol, zacc)

    per_q = tq // tk
    carry = lax.fori_loop(0, qi * per_q, lambda kb, c: tile(kb, c, None), carry)
    row = lax.broadcasted_iota(jnp.int32, (tq, tk), 0)
    col = lax.broadcasted_iota(jnp.int32, (tq, tk), 1)
    for d in range(per_q):
        mask = (col + d * tk) <= row
        carry = tile(qi * per_q + d, carry, mask)
    m1, l1, a1, m2, l2, a2 = carry

    lp = lam_ref[...]
    e1 = jnp.exp(jnp.sum(lp[0:1, :] * lp[1:2, :], axis=1, keepdims=True))
    e2 = jnp.exp(jnp.sum(lp[2:3, :] * lp[3:4, :], axis=1, keepdims=True))
    lam = e1 - e2 + lam_init
    o = a1 / l1 - lam * (a2 / l2)
    ms = jnp.mean(o * o, axis=-1, keepdims=True)
    o = o * lax.rsqrt(ms + NORM_EPS) * subln_ref[...] * (1.0 - lam_init)
    o_ref[...] = o.astype(BF16)


def _diff_attn(proj, lam_p, subln, lam_init):
    b, s, _ = proj.shape
    nq = s // DIFF_TQ
    kcol0 = DIFF_WIDTH // LANES
    vcol0 = 2 * DIFF_WIDTH // LANES
    return pl.pallas_call(
        functools.partial(_diff_attn_kernel, lam_init=lam_init),
        grid=(b, DIFF_HEADS, nq),
        in_specs=[
            pl.BlockSpec((None, DIFF_TQ, LANES), lambda bi, h, qi: (bi, qi, h)),
            pl.BlockSpec((None, s, LANES), lambda bi, h, qi: (bi, 0, kcol0 + h)),
            pl.BlockSpec((None, s, LANES), lambda bi, h, qi: (bi, 0, vcol0 + h)),
            pl.BlockSpec((4, HEAD_DIM), lambda bi, h, qi: (0, 0)),
            pl.BlockSpec((1, LANES), lambda bi, h, qi: (0, 0)),
        ],
        out_specs=pl.BlockSpec((None, DIFF_TQ, LANES), lambda bi, h, qi: (bi, qi, h)),
        out_shape=jax.ShapeDtypeStruct((b, s, DIFF_WIDTH), BF16),
        compiler_params=pltpu.CompilerParams(
            dimension_semantics=("arbitrary", "arbitrary", "arbitrary"),
            vmem_limit_bytes=VMEM_LIMIT),
        name="diff_attn",
    )(proj, proj, proj, lam_p, subln)


def _sb_attn_kernel(q_ref, k_ref, v_ref, o_ref):
    tq, tk = SB_TQ, SB_TK
    qi = pl.program_id(2)
    q = q_ref[...]
    lane = lax.broadcasted_iota(jnp.int32, q.shape, 1)
    zero = jnp.zeros_like(q)
    per_q = tq // tk
    jj = lax.broadcasted_iota(jnp.int32, (tk, tk), 0)
    ss = lax.broadcasted_iota(jnp.int32, (tk, tk), 1)
    suffix = jnp.where(jj > ss, 1.0, 0.0).astype(BF16)
    row = lax.broadcasted_iota(jnp.int32, (tq, tk), 0)
    col = lax.broadcasted_iota(jnp.int32, (tq, tk), 1)

    def tile(qm, kb, carry, mask):
        rem, acc = carry
        start = pl.multiple_of(kb * tk, tk)
        k = k_ref[pl.ds(start, tk), :]
        v = v_ref[pl.ds(start, tk), :]
        z = _nt_dot(qm, k)
        sp = jnp.maximum(z, 0.0) + jnp.log(1.0 + jnp.exp(-jnp.abs(z)))
        log_rem = -sp
        if mask is not None:
            log_rem = jnp.where(mask, log_rem, 0.0)
        log_beta = z - sp
        hi = log_rem.astype(BF16)
        lo = (log_rem - hi.astype(F32)).astype(BF16)
        after = (jnp.dot(hi, suffix, preferred_element_type=F32)
                 + jnp.dot(lo, suffix, preferred_element_type=F32))
        a = jnp.exp(log_beta + after + rem)
        if mask is not None:
            a = jnp.where(mask, a, 0.0)
        acc = acc + jnp.dot(a.astype(BF16), v, preferred_element_type=F32)
        rem = rem + jnp.sum(log_rem, axis=1, keepdims=True)
        return rem, acc

    def head(qm):
        carry = (jnp.zeros((tq, 1), F32), jnp.zeros((tq, LANES), F32))
        for d in reversed(range(per_q)):
            mask = (col + d * tk) < row
            carry = tile(qm, qi * per_q + d, carry, mask)
        n_full = qi * per_q
        carry = lax.fori_loop(
            0, n_full, lambda i, c: tile(qm, n_full - 1 - i, c, None), carry)
        return carry[1]

    acc_lo = head(jnp.where(lane < HEAD_DIM, q, zero))
    acc_hi = head(jnp.where(lane >= HEAD_DIM, q, zero))
    o_ref[...] = jnp.where(lane < HEAD_DIM, acc_lo, acc_hi).astype(BF16)


def _sb_attn(proj):
    b, s, _ = proj.shape
    nq = s // SB_TQ
    qcol0 = 3 * DIFF_WIDTH // LANES
    kcol0 = qcol0 + SB_PAIRS
    vcol0 = kcol0 + SB_PAIRS
    return pl.pallas_call(
        _sb_attn_kernel,
        grid=(b, SB_PAIRS, nq),
        in_specs=[
            pl.BlockSpec((None, SB_TQ, LANES), lambda bi, p, qi: (bi, qi, qcol0 + p)),
            pl.BlockSpec((None, s, LANES), lambda bi, p, qi: (bi, 0, kcol0 + p)),
            pl.BlockSpec((None, s, LANES), lambda bi, p, qi: (bi, 0, vcol0 + p)),
        ],
        out_specs=pl.BlockSpec((None, SB_TQ, LANES), lambda bi, p, qi: (bi, qi, p)),
        out_shape=jax.ShapeDtypeStruct((b, s, SB_WIDTH), BF16),
        compiler_params=pltpu.CompilerParams(
            dimension_semantics=("arbitrary", "arbitrary", "arbitrary"),
            vmem_limit_bytes=VMEM_LIMIT),
        name="sb_attn",
    )(proj, proj, proj)


def _merge_kernel(x_ref, od_ref, os_ref, gd_ref, gs_ref, wd_ref, ws_ref, wo_ref, o_ref):
    pd = jnp.dot(od_ref[...], wd_ref[...], preferred_element_type=F32)
    ps = jnp.dot(os_ref[...], ws_ref[...], preferred_element_type=F32)
    merged = gd_ref[...].astype(F32) * pd + gs_ref[...].astype(F32) * ps
    o_ref[...] = x_ref[...] + jnp.dot(merged.astype(BF16), wo_ref[...],
                                      preferred_element_type=F32)


def _merge(x2d, o_diff, o_sb, proj2d, w_o_diff, w_o_sb, w_out):
    t_rows = x2d.shape[0]
    tm = MERGE_TM
    gd_blk = GATE_COL0 // D_MODEL
    const = lambda i: (0, 0)
    return pl.pallas_call(
        _merge_kernel,
        grid=(t_rows // tm,),
        in_specs=[
            pl.BlockSpec((tm, D_MODEL), lambda i: (i, 0)),
            pl.BlockSpec((tm, DIFF_WIDTH), lambda i: (i, 0)),
            pl.BlockSpec((tm, SB_WIDTH), lambda i: (i, 0)),
            pl.BlockSpec((tm, D_MODEL), lambda i: (i, gd_blk)),
            pl.BlockSpec((tm, D_MODEL), lambda i: (i, gd_blk + 1)),
            pl.BlockSpec((DIFF_WIDTH, D_MODEL), const),
            pl.BlockSpec((SB_WIDTH, D_MODEL), const),
            pl.BlockSpec((D_MODEL, D_MODEL), const),
        ],
        out_specs=pl.BlockSpec((tm, D_MODEL), lambda i: (i, 0)),
        out_shape=jax.ShapeDtypeStruct((t_rows, D_MODEL), F32),
        compiler_params=pltpu.CompilerParams(
            dimension_semantics=("arbitrary",), vmem_limit_bytes=VMEM_LIMIT),
        name="merge",
    )(x2d, o_diff, o_sb, proj2d, proj2d, w_o_diff, w_o_sb, w_out)


def _ffn_kernel(x_ref, g_ref, wg_ref, wu_ref, wd_ref, gf_ref, o_ref, h_ref, acc_ref,
                *, final_norm):
    c = pl.program_id(1)

    @pl.when(c == 0)
    def _():
        x = x_ref[...]
        ms = jnp.mean(x * x, axis=-1, keepdims=True)
        h_ref[...] = (x * lax.rsqrt(ms + NORM_EPS) * g_ref[...]).astype(BF16)
        acc_ref[...] = x

    h = h_ref[...]
    gate = jnp.dot(h, wg_ref[...], preferred_element_type=F32)
    up = jnp.dot(h, wu_ref[...], preferred_element_type=F32)
    act = (gate * jax.nn.sigmoid(gate) * up).astype(BF16)
    acc_ref[...] += jnp.dot(act, wd_ref[...], preferred_element_type=F32)

    @pl.when(c == pl.num_programs(1) - 1)
    def _():
        y = acc_ref[...]
        if final_norm:
            ms = jnp.mean(y * y, axis=-1, keepdims=True)
            y = y * lax.rsqrt(ms + NORM_EPS) * gf_ref[...]
        o_ref[...] = y


def _ffn(x2d, g, w_in, w_out, g_final, final_norm):
    t_rows = x2d.shape[0]
    tm, th = FFN_TM, FFN_TH
    n_c = FFN_HIDDEN // th
    return pl.pallas_call(
        functools.partial(_ffn_kernel, final_norm=final_norm),
        grid=(t_rows // tm, n_c),
        in_specs=[
            pl.BlockSpec((tm, D_MODEL), lambda i, c: (i, 0)),
            pl.BlockSpec((1, D_MODEL), lambda i, c: (0, 0)),
            pl.BlockSpec((D_MODEL, th), lambda i, c: (0, c)),
            pl.BlockSpec((D_MODEL, th), lambda i, c: (0, n_c + c)),
            pl.BlockSpec((th, D_MODEL), lambda i, c: (c, 0)),
            pl.BlockSpec((1, D_MODEL), lambda i, c: (0, 0)),
        ],
        out_specs=pl.BlockSpec((tm, D_MODEL), lambda i, c: (i, 0)),
        out_shape=jax.ShapeDtypeStruct((t_rows, D_MODEL), F32),
        scratch_shapes=[pltpu.VMEM((tm, D_MODEL), BF16), pltpu.VMEM((tm, D_MODEL), F32)],
        compiler_params=pltpu.CompilerParams(
            dimension_semantics=("arbitrary", "arbitrary"), vmem_limit_bytes=VMEM_LIMIT),
        name="ffn",
    )(x2d, g, w_in, w_in, w_out, g_final)


def _rope_tables(seq):
    half = HEAD_DIM // 2
    pos = jnp.arange(seq, dtype=F32)
    inv = ROPE_THETA ** (-jnp.arange(0, HEAD_DIM, 2, dtype=F32) / HEAD_DIM)
    ang = pos[:, None] * inv[None, :]
    cos, sin = jnp.cos(ang), jnp.sin(ang)
    cos_t = jnp.tile(cos, (1, LANES // half))
    sin_t = jnp.tile(jnp.concatenate([-sin, sin], axis=1), (1, LANES // HEAD_DIM))
    return cos_t, sin_t


def kernel(x, norm_attn, w_in, b_gate, diff_lambda, diff_subln, w_o_diff, w_o_sb, w_out,
           norm_ffn, w_ffn_in, w_ffn_out, norm_final):
    b, s, d = x.shape
    cos_t, sin_t = _rope_tables(s)
    x2d = x.reshape(b * s, d)
    g_final = norm_final.reshape(1, d)
    for layer in range(DEPTH):
        lam_init = 0.8 - 0.6 * math.exp(-0.3 * layer)
        bg_full = jnp.concatenate(
            [jnp.zeros((GATE_COL0,), F32), b_gate[layer]]).reshape(1, IN_COLS)
        proj2d = _in_proj(x2d, norm_attn[layer].reshape(1, d), w_in[layer].astype(BF16),
                          bg_full, cos_t, sin_t, s)
        proj = proj2d.reshape(b, s, IN_COLS)
        o_diff = _diff_attn(proj, diff_lambda[layer], diff_subln[layer].reshape(1, LANES),
                            lam_init)
        o_sb = _sb_attn(proj)
        x2d = _merge(x2d, o_diff.reshape(b * s, DIFF_WIDTH), o_sb.reshape(b * s, SB_WIDTH),
                     proj2d, w_o_diff[layer].astype(BF16), w_o_sb[layer].astype(BF16),
                     w_out[layer].astype(BF16))
        x2d = _ffn(x2d, norm_ffn[layer].reshape(1, d), w_ffn_in[layer].astype(BF16),
                   w_ffn_out[layer].astype(BF16), g_final, layer == DEPTH - 1)
    return x2d.reshape(b, s, d)
```

```python
import functools
import math

import jax
import jax.numpy as jnp
from jax import lax
from jax.experimental import pallas as pl
from jax.experimental.pallas import tpu as pltpu

D_MODEL = 1024
DEPTH = 2
HEAD_DIM = 64
LANES = 128
DIFF_HEADS = 4
DIFF_WIDTH = DIFF_HEADS * 2 * HEAD_DIM
SB_HEADS = 8
SB_WIDTH = SB_HEADS * HEAD_DIM
SB_PAIRS = SB_WIDTH // LANES
IN_COLS = 3 * DIFF_WIDTH + 3 * SB_WIDTH + 2 * D_MODEL
GATE_COL0 = 3 * DIFF_WIDTH + 3 * SB_WIDTH
FFN_HIDDEN = 2816
ROPE_THETA = 10000.0
NORM_EPS = 1e-6
NEG_BIG = -1e30
LOG2E = math.log2(math.e)

VMEM_LIMIT = 48 * 1024 * 1024

IN_TM, IN_TN = 1024, 512
DIFF_TQ, DIFF_TK = 512, 512
SB_TQ, SB_TK, SB_RC = 512, 256, 256
MERGE_TM = 512
FFN_TM, FFN_TH = 1024, 1408

BF16 = jnp.bfloat16
F32 = jnp.float32


def _nt_dot(a, b):
    return lax.dot_general(a, b, (((1,), (1,)), ((), ())), preferred_element_type=F32)


def _in_proj_kernel(x_ref, g_ref, w_ref, bg_ref, cos_ref, sin_ref, o_ref, h_ref):
    j = pl.program_id(1)

    @pl.when(j == 0)
    def _():
        x = x_ref[...]
        ms = jnp.mean(x * x, axis=-1, keepdims=True)
        h_ref[...] = (x * lax.rsqrt(ms + NORM_EPS) * g_ref[...]).astype(BF16)

    acc = jnp.dot(h_ref[...], w_ref[...], preferred_element_type=F32)

    n_diff_q = DIFF_WIDTH // IN_TN
    n_rope = 2 * DIFF_WIDTH // IN_TN
    sb_q0 = 3 * DIFF_WIDTH // IN_TN
    sb_q1 = (3 * DIFF_WIDTH + SB_WIDTH) // IN_TN
    gate0 = GATE_COL0 // IN_TN
    scale = HEAD_DIM ** -0.5 * LOG2E

    @pl.when(j < n_rope)
    def _():
        cos = cos_ref[...]
        sin = sin_ref[...]
        lane = lax.broadcasted_iota(jnp.int32, cos.shape, 1)
        first_half = (lane % HEAD_DIM) < (HEAD_DIM // 2)
        sc = jnp.where(j < n_diff_q, scale, 1.0).astype(F32)
        for c in range(IN_TN // LANES):
            t = acc[:, c * LANES:(c + 1) * LANES]
            partner = jnp.where(first_half,
                                pltpu.roll(t, LANES - HEAD_DIM // 2, 1),
                                pltpu.roll(t, HEAD_DIM // 2, 1))
            r = (t * cos + partner * sin) * sc
            o_ref[:, c * LANES:(c + 1) * LANES] = r.astype(BF16)

    @pl.when((j >= n_rope) & (j < gate0))
    def _():
        sc = jnp.where((j >= sb_q0) & (j < sb_q1), scale, 1.0).astype(F32)
        o_ref[...] = (acc * sc).astype(BF16)

    @pl.when(j >= gate0)
    def _():
        o_ref[...] = jax.nn.sigmoid(acc + bg_ref[...]).astype(BF16)


def _in_proj(x2d, g, w, bg_full, cos_t, sin_t, seq):
    t_rows = x2d.shape[0]
    n_i, n_j = t_rows // IN_TM, IN_COLS // IN_TN
    pos_tiles = seq // IN_TM
    return pl.pallas_call(
        _in_proj_kernel,
        grid=(n_i, n_j),
        in_specs=[
            pl.BlockSpec((IN_TM, D_MODEL), lambda i, j: (i, 0)),
            pl.BlockSpec((1, D_MODEL), lambda i, j: (0, 0)),
            pl.BlockSpec((D_MODEL, IN_TN), lambda i, j: (0, j)),
            pl.BlockSpec((1, IN_TN), lambda i, j: (0, j)),
            pl.BlockSpec((IN_TM, LANES), lambda i, j: (i % pos_tiles, 0)),
            pl.BlockSpec((IN_TM, LANES), lambda i, j: (i % pos_tiles, 0)),
        ],
        out_specs=pl.BlockSpec((IN_TM, IN_TN), lambda i, j: (i, j)),
        out_shape=jax.ShapeDtypeStruct((t_rows, IN_COLS), BF16),
        scratch_shapes=[pltpu.VMEM((IN_TM, D_MODEL), BF16)],
        compiler_params=pltpu.CompilerParams(
            dimension_semantics=("arbitrary", "arbitrary"), vmem_limit_bytes=VMEM_LIMIT),
        name="in_proj",
    )(x2d, g, w, bg_full, cos_t, sin_t)


def _diff_attn_kernel(q_ref, k_ref, v_ref, lam_ref, subln_ref, o_ref, *, lam_init):
    tq, tk = DIFF_TQ, DIFF_TK
    qi = pl.program_id(2)
    q = q_ref[...]
    lane = lax.broadcasted_iota(jnp.int32, q.shape, 1)
    zero = jnp.zeros_like(q)
    q1 = jnp.where(lane < HEAD_DIM, q, zero)
    q2 = jnp.where(lane >= HEAD_DIM, q, zero)

    def softmax_step(qm, k, v, m, l, a, mask):
        s = _nt_dot(qm, k)
        if mask is not None:
            s = jnp.where(mask, s, NEG_BIG)
        m_new = jnp.maximum(m, jnp.max(s, axis=1, keepdims=True))
        alpha = jnp.exp2(m - m_new)
        p = jnp.exp2(s - m_new)
        l = alpha * l + jnp.sum(p, axis=1, keepdims=True)
        a = alpha * a + jnp.dot(p.astype(BF16), v, preferred_element_type=F32)
        return m_new, l, a

    def tile(kb, carry, mask):
        m1, l1, a1, m2, l2, a2 = carry
        start = pl.multiple_of(kb * tk, tk)
        k = k_ref[pl.ds(start, tk), :]
        v = v_ref[pl.ds(start, tk), :]
        m1, l1, a1 = softmax_step(q1, k, v, m1, l1, a1, mask)
        m2, l2, a2 = softmax_step(q2, k, v, m2, l2, a2, mask)
        return m1, l1, a1, m2, l2, a2

    col0 = jnp.full((tq, 1), NEG_BIG, F32)
    zcol = jnp.zeros((tq, 1), F32)
    zacc = jnp.zeros((tq, LANES), F32)
    carry = (col0, zcol, zacc, col0, zcol, zacc)

    per_q = tq // tk
    carry = lax.fori_loop(0, qi * per_q, lambda kb, c: tile(kb, c, None), carry)
    row = lax.broadcasted_iota(jnp.int32, (tq, tk), 0)
    col = lax.broadcasted_iota(jnp.int32, (tq, tk), 1)
    for d in range(per_q):
        mask = (col + d * tk) <= row
        carry = tile(qi * per_q + d, carry, mask)
    m1, l1, a1, m2, l2, a2 = carry

    lp = lam_ref[...]
    e1 = jnp.exp(jnp.sum(lp[0:1, :] * lp[1:2, :], axis=1, keepdims=True))
    e2 = jnp.exp(jnp.sum(lp[2:3, :] * lp[3:4, :], axis=1, keepdims=True))
    lam = e1 - e2 + lam_init
    o = a1 / l1 - lam * (a2 / l2)
    ms = jnp.mean(o * o, axis=-1, keepdims=True)
    o = o * lax.rsqrt(ms + NORM_EPS) * subln_ref[...] * (1.0 - lam_init)
    o_ref[...] = o.astype(BF16)


def _diff_attn(proj, lam_p, subln, lam_init):
    b, s, _ = proj.shape
    nq = s // DIFF_TQ
    kcol0 = DIFF_WIDTH // LANES
    vcol0 = 2 * DIFF_WIDTH // LANES
    return pl.pallas_call(
        functools.partial(_diff_attn_kernel, lam_init=lam_init),
        grid=(b, DIFF_HEADS, nq),
        in_specs=[
            pl.BlockSpec((None, DIFF_TQ, LANES), lambda bi, h, qi: (bi, qi, h)),
            pl.BlockSpec((None, s, LANES), lambda bi, h, qi: (bi, 0, kcol0 + h)),
            pl.BlockSpec((None, s, LANES), lambda bi, h, qi: (bi, 0, vcol0 + h)),
            pl.BlockSpec((4, HEAD_DIM), lambda bi, h, qi: (0, 0)),
            pl.BlockSpec((1, LANES), lambda bi, h, qi: (0, 0)),
        ],
        out_specs=pl.BlockSpec((None, DIFF_TQ, LANES), lambda bi, h, qi: (bi, qi, h)),
        out_shape=jax.ShapeDtypeStruct((b, s, DIFF_WIDTH), BF16),
        compiler_params=pltpu.CompilerParams(
            dimension_semantics=("arbitrary", "arbitrary", "arbitrary"),
            vmem_limit_bytes=VMEM_LIMIT),
        name="diff_attn",
    )(proj, proj, proj, lam_p, subln)


def _sb_attn_kernel(q_ref, k_ref, v_ref, o_ref, buf0, buf1):
    tq, tk = SB_TQ, SB_TK
    qi = pl.program_id(2)
    q = q_ref[...]
    lane = lax.broadcasted_iota(jnp.int32, q.shape, 1)
    zero = jnp.zeros_like(q)
    per_q = tq // tk
    qs = jnp.concatenate([jnp.where(lane < HEAD_DIM, q, zero),
                          jnp.where(lane >= HEAD_DIM, q, zero)], axis=0)
    jj = lax.broadcasted_iota(jnp.int32, (2 * tk, tk), 0) % tk
    ss = lax.broadcasted_iota(jnp.int32, (2 * tk, tk), 1)
    neg_suffix = jnp.where(jj >= ss, -1.0, 0.0).astype(BF16)
    sign_bit = jnp.uint32(0x80000000)
    rc = SB_RC
    chunks = [slice(c * rc, (c + 1) * rc) for c in range(2 * tq // rc)]
    row = lax.broadcasted_iota(jnp.int32, (rc, tk), 0)
    col = lax.broadcasted_iota(jnp.int32, (rc, tk), 1)

    def scores(rows, k, diag):
        row0 = rows.start % tq
        if diag is not None and row0 + rc <= diag + 1:
            return jnp.full((rc, tk), NEG_BIG, F32), jnp.zeros((rc, 1), F32)
        if diag is not None and row0 >= diag + tk:
            diag = None
        z = _nt_dot(qs[rows], k)
        neg_abs = lax.bitcast_convert_type(
            lax.bitcast_convert_type(z, jnp.uint32) | sign_bit, F32)
        sp = jnp.maximum(z, 0.0) + jnp.log2(1.0 + jnp.exp2(neg_abs))
        if diag is not None:
            mask = (col + diag) < (row + row0)
            sp = jnp.where(mask, sp, 0.0)
        hi = sp.astype(BF16)
        lo = (sp - hi.astype(F32)).astype(BF16)
        arg = z + jnp.dot(jnp.concatenate([hi, lo], axis=1), neg_suffix,
                          preferred_element_type=F32)
        if diag is not None:
            arg = jnp.where(mask, arg, NEG_BIG)
        return arg, jnp.sum(sp, axis=1, keepdims=True)

    def accumulate(arg, rem, acc, v):
        a = jnp.exp2(arg + rem)
        return acc + jnp.dot(a.astype(BF16), v, preferred_element_type=F32)

    def step(kb_next, diag, buf_in, buf_out, carry):
        rem, acc, row_sum = carry
        k = k_ref[pl.ds(pl.multiple_of(kb_next * tk, tk), tk), :]
        v = v_ref[pl.ds(pl.multiple_of((kb_next + 1) * tk, tk), tk), :]
        rem_n, acc_n, row_sum_n = [], [], []
        for c, rows in enumerate(chunks):
            arg_next, rs = scores(rows, k, diag)
            buf_out[rows, :] = arg_next
            acc_n.append(accumulate(buf_in[rows, :], rem[c], acc[c], v))
            rem_n.append(rem[c] - row_sum[c])
            row_sum_n.append(rs)
        return rem_n, acc_n, row_sum_n

    assert per_q == 2
    d1, d0 = qi * per_q + 1, qi * per_q
    k1 = k_ref[pl.ds(pl.multiple_of(d1 * tk, tk), tk), :]
    row_sum = []
    for rows in chunks:
        arg, rs = scores(rows, k1, tk)
        buf1[rows, :] = arg
        row_sum.append(rs)
    carry = ([jnp.zeros((rc, 1), F32)] * len(chunks),
             [jnp.zeros((rc, LANES), F32)] * len(chunks), row_sum)
    carry = step(d0, 0, buf1, buf0, carry)

    def pair(j, carry):
        kb = d0 - 1 - 2 * j
        carry = step(kb, None, buf0, buf1, carry)
        return step(kb - 1, None, buf1, buf0, carry)

    rem, acc, _ = lax.fori_loop(0, qi, pair, carry)
    v0 = v_ref[pl.ds(0, tk), :]
    acc = jnp.concatenate([accumulate(buf0[rows, :], rem[c], acc[c], v0)
                           for c, rows in enumerate(chunks)], axis=0)
    o_ref[...] = jnp.where(lane < HEAD_DIM, acc[:tq], acc[tq:]).astype(BF16)


def _sb_attn(proj):
    b, s, _ = proj.shape
    nq = s // SB_TQ
    qcol0 = 3 * DIFF_WIDTH // LANES
    kcol0 = qcol0 + SB_PAIRS
    vcol0 = kcol0 + SB_PAIRS
    return pl.pallas_call(
        _sb_attn_kernel,
        grid=(b, SB_PAIRS, nq),
        in_specs=[
            pl.BlockSpec((None, SB_TQ, LANES), lambda bi, p, qi: (bi, qi, qcol0 + p)),
            pl.BlockSpec((None, s, LANES), lambda bi, p, qi: (bi, 0, kcol0 + p)),
            pl.BlockSpec((None, s, LANES), lambda bi, p, qi: (bi, 0, vcol0 + p)),
        ],
        out_specs=pl.BlockSpec((None, SB_TQ, LANES), lambda bi, p, qi: (bi, qi, p)),
        out_shape=jax.ShapeDtypeStruct((b, s, SB_WIDTH), BF16),
        scratch_shapes=[pltpu.VMEM((2 * SB_TQ, SB_TK), F32)] * 2,
        compiler_params=pltpu.CompilerParams(
            dimension_semantics=("arbitrary", "arbitrary", "arbitrary"),
            vmem_limit_bytes=VMEM_LIMIT),
        name="sb_attn",
    )(proj, proj, proj)


def _merge_kernel(x_ref, od_ref, os_ref, gd_ref, gs_ref, wd_ref, ws_ref, wo_ref, o_ref):
    pd = jnp.dot(od_ref[...], wd_ref[...], preferred_element_type=F32)
    ps = jnp.dot(os_ref[...], ws_ref[...], preferred_element_type=F32)
    merged = gd_ref[...].astype(F32) * pd + gs_ref[...].astype(F32) * ps
    o_ref[...] = x_ref[...] + jnp.dot(merged.astype(BF16), wo_ref[...],
                                      preferred_element_type=F32)


def _merge(x2d, o_diff, o_sb, proj2d, w_o_diff, w_o_sb, w_out):
    t_rows = x2d.shape[0]
    tm = MERGE_TM
    gd_blk = GATE_COL0 // D_MODEL
    const = lambda i: (0, 0)
    return pl.pallas_call(
        _merge_kernel,
        grid=(t_rows // tm,),
        in_specs=[
            pl.BlockSpec((tm, D_MODEL), lambda i: (i, 0)),
            pl.BlockSpec((tm, DIFF_WIDTH), lambda i: (i, 0)),
            pl.BlockSpec((tm, SB_WIDTH), lambda i: (i, 0)),
            pl.BlockSpec((tm, D_MODEL), lambda i: (i, gd_blk)),
            pl.BlockSpec((tm, D_MODEL), lambda i: (i, gd_blk + 1)),
            pl.BlockSpec((DIFF_WIDTH, D_MODEL), const),
            pl.BlockSpec((SB_WIDTH, D_MODEL), const),
            pl.BlockSpec((D_MODEL, D_MODEL), const),
        ],
        out_specs=pl.BlockSpec((tm, D_MODEL), lambda i: (i, 0)),
        out_shape=jax.ShapeDtypeStruct((t_rows, D_MODEL), F32),
        compiler_params=pltpu.CompilerParams(
            dimension_semantics=("arbitrary",), vmem_limit_bytes=VMEM_LIMIT),
        name="merge",
    )(x2d, o_diff, o_sb, proj2d, proj2d, w_o_diff, w_o_sb, w_out)


def _ffn_kernel(x_ref, g_ref, wg_ref, wu_ref, wd_ref, gf_ref, o_ref, h_ref, acc_ref,
                *, final_norm):
    c = pl.program_id(1)

    @pl.when(c == 0)
    def _():
        x = x_ref[...]
        ms = jnp.mean(x * x, axis=-1, keepdims=True)
        h_ref[...] = (x * lax.rsqrt(ms + NORM_EPS) * g_ref[...]).astype(BF16)
        acc_ref[...] = x

    h = h_ref[...]
    gate = jnp.dot(h, wg_ref[...], preferred_element_type=F32)
    up = jnp.dot(h, wu_ref[...], preferred_element_type=F32)
    act = (gate * jax.nn.sigmoid(gate) * up).astype(BF16)
    acc_ref[...] += jnp.dot(act, wd_ref[...], preferred_element_type=F32)

    @pl.when(c == pl.num_programs(1) - 1)
    def _():
        y = acc_ref[...]
        if final_norm:
            ms = jnp.mean(y * y, axis=-1, keepdims=True)
            y = y * lax.rsqrt(ms + NORM_EPS) * gf_ref[...]
        o_ref[...] = y


def _ffn(x2d, g, w_in, w_out, g_final, final_norm):
    t_rows = x2d.shape[0]
    tm, th = FFN_TM, FFN_TH
    n_c = FFN_HIDDEN // th
    return pl.pallas_call(
        functools.partial(_ffn_kernel, final_norm=final_norm),
        grid=(t_rows // tm, n_c),
        in_specs=[
            pl.BlockSpec((tm, D_MODEL), lambda i, c: (i, 0)),
            pl.BlockSpec((1, D_MODEL), lambda i, c: (0, 0)),
            pl.BlockSpec((D_MODEL, th), lambda i, c: (0, c)),
            pl.BlockSpec((D_MODEL, th), lambda i, c: (0, n_c + c)),
            pl.BlockSpec((th, D_MODEL), lambda i, c: (c, 0)),
            pl.BlockSpec((1, D_MODEL), lambda i, c: (0, 0)),
        ],
        out_specs=pl.BlockSpec((tm, D_MODEL), lambda i, c: (i, 0)),
        out_shape=jax.ShapeDtypeStruct((t_rows, D_MODEL), F32),
        scratch_shapes=[pltpu.VMEM((tm, D_MODEL), BF16), pltpu.VMEM((tm, D_MODEL), F32)],
        compiler_params=pltpu.CompilerParams(
            dimension_semantics=("arbitrary", "arbitrary"), vmem_limit_bytes=VMEM_LIMIT),
        name="ffn",
    )(x2d, g, w_in, w_in, w_out, g_final)


def _rope_tables(seq):
    half = HEAD_DIM // 2
    pos = jnp.arange(seq, dtype=F32)
    inv = ROPE_THETA ** (-jnp.arange(0, HEAD_DIM, 2, dtype=F32) / HEAD_DIM)
    ang = pos[:, None] * inv[None, :]
    cos, sin = jnp.cos(ang), jnp.sin(ang)
    cos_t = jnp.tile(cos, (1, LANES // half))
    sin_t = jnp.tile(jnp.concatenate([-sin, sin], axis=1), (1, LANES // HEAD_DIM))
    return cos_t, sin_t


def kernel(x, norm_attn, w_in, b_gate, diff_lambda, diff_subln, w_o_diff, w_o_sb, w_out,
           norm_ffn, w_ffn_in, w_ffn_out, norm_final):
    b, s, d = x.shape
    cos_t, sin_t = _rope_tables(s)
    x2d = x.reshape(b * s, d)
    g_final = norm_final.reshape(1, d)
    for layer in range(DEPTH):
        lam_init = 0.8 - 0.6 * math.exp(-0.3 * layer)
        bg_full = jnp.concatenate(
            [jnp.zeros((GATE_COL0,), F32), b_gate[layer]]).reshape(1, IN_COLS)
        proj2d = _in_proj(x2d, norm_attn[layer].reshape(1, d), w_in[layer].astype(BF16),
                          bg_full, cos_t, sin_t, s)
        proj = proj2d.reshape(b, s, IN_COLS)
        o_diff = _diff_attn(proj, diff_lambda[layer], diff_subln[layer].reshape(1, LANES),
                            lam_init)
        o_sb = _sb_attn(proj)
        x2d = _merge(x2d, o_diff.reshape(b * s, DIFF_WIDTH), o_sb.reshape(b * s, SB_WIDTH),
                     proj2d, w_o_diff[layer].astype(BF16), w_o_sb[layer].astype(BF16),
                     w_out[layer].astype(BF16))
        x2d = _ffn(x2d, norm_ffn[layer].reshape(1, d), w_ffn_in[layer].astype(BF16),
                   w_ffn_out[layer].astype(BF16), g_final, layer == DEPTH - 1)
    return x2d.reshape(b, s, d)
```

```python
import functools
import math

import jax
import jax.numpy as jnp
from jax import lax
from jax.experimental import pallas as pl
from jax.experimental.pallas import tpu as pltpu

D_MODEL = 1024
DEPTH = 2
HEAD_DIM = 64
LANES = 128
DIFF_HEADS = 4
DIFF_WIDTH = DIFF_HEADS * 2 * HEAD_DIM
SB_HEADS = 8
SB_WIDTH = SB_HEADS * HEAD_DIM
SB_PAIRS = SB_WIDTH // LANES
IN_COLS = 3 * DIFF_WIDTH + 3 * SB_WIDTH + 2 * D_MODEL
GATE_COL0 = 3 * DIFF_WIDTH + 3 * SB_WIDTH
FFN_HIDDEN = 2816
ROPE_THETA = 10000.0
NORM_EPS = 1e-6
NEG_BIG = -1e30
LOG2E = math.log2(math.e)

VMEM_LIMIT = 48 * 1024 * 1024

IN_TM, IN_TN = 1024, 512
DIFF_TQ, DIFF_TK, DIFF_CC = 512, 256, 256
SB_TQ, SB_TK, SB_CC = 512, 256, 256
MERGE_TM = 512
FFN_TM, FFN_TH = 1024, 1408

BF16 = jnp.bfloat16
F32 = jnp.float32


def _nt_dot(a, b):
    return lax.dot_general(a, b, (((1,), (1,)), ((), ())), preferred_element_type=F32)


def _in_proj_kernel(x_ref, g_ref, w_ref, bg_ref, cos_ref, sin_ref, o_ref, h_ref):
    j = pl.program_id(1)

    @pl.when(j == 0)
    def _():
        x = x_ref[...]
        ms = jnp.mean(x * x, axis=-1, keepdims=True)
        h_ref[...] = (x * lax.rsqrt(ms + NORM_EPS) * g_ref[...]).astype(BF16)

    acc = jnp.dot(h_ref[...], w_ref[...], preferred_element_type=F32)

    n_diff_q = DIFF_WIDTH // IN_TN
    n_rope = 2 * DIFF_WIDTH // IN_TN
    sb_q0 = 3 * DIFF_WIDTH // IN_TN
    sb_q1 = (3 * DIFF_WIDTH + SB_WIDTH) // IN_TN
    gate0 = GATE_COL0 // IN_TN
    scale = HEAD_DIM ** -0.5 * LOG2E

    @pl.when(j < n_rope)
    def _():
        cos = cos_ref[...]
        sin = sin_ref[...]
        lane = lax.broadcasted_iota(jnp.int32, cos.shape, 1)
        first_half = (lane % HEAD_DIM) < (HEAD_DIM // 2)
        sc = jnp.where(j < n_diff_q, scale, 1.0).astype(F32)
        for c in range(IN_TN // LANES):
            t = acc[:, c * LANES:(c + 1) * LANES]
            partner = jnp.where(first_half,
                                pltpu.roll(t, LANES - HEAD_DIM // 2, 1),
                                pltpu.roll(t, HEAD_DIM // 2, 1))
            r = (t * cos + partner * sin) * sc
            o_ref[:, c * LANES:(c + 1) * LANES] = r.astype(BF16)

    @pl.when((j >= n_rope) & (j < gate0))
    def _():
        sc = jnp.where((j >= sb_q0) & (j < sb_q1), scale, 1.0).astype(F32)
        o_ref[...] = (acc * sc).astype(BF16)

    @pl.when(j >= gate0)
    def _():
        o_ref[...] = jax.nn.sigmoid(acc + bg_ref[...]).astype(BF16)


def _in_proj(x2d, g, w, bg_full, cos_t, sin_t, seq):
    t_rows = x2d.shape[0]
    n_i, n_j = t_rows // IN_TM, IN_COLS // IN_TN
    pos_tiles = seq // IN_TM
    return pl.pallas_call(
        _in_proj_kernel,
        grid=(n_i, n_j),
        in_specs=[
            pl.BlockSpec((IN_TM, D_MODEL), lambda i, j: (i, 0)),
            pl.BlockSpec((1, D_MODEL), lambda i, j: (0, 0)),
            pl.BlockSpec((D_MODEL, IN_TN), lambda i, j: (0, j)),
            pl.BlockSpec((1, IN_TN), lambda i, j: (0, j)),
            pl.BlockSpec((IN_TM, LANES), lambda i, j: (i % pos_tiles, 0)),
            pl.BlockSpec((IN_TM, LANES), lambda i, j: (i % pos_tiles, 0)),
        ],
        out_specs=pl.BlockSpec((IN_TM, IN_TN), lambda i, j: (i, j)),
        out_shape=jax.ShapeDtypeStruct((t_rows, IN_COLS), BF16),
        scratch_shapes=[pltpu.VMEM((IN_TM, D_MODEL), BF16)],
        compiler_params=pltpu.CompilerParams(
            dimension_semantics=("arbitrary", "arbitrary"), vmem_limit_bytes=VMEM_LIMIT),
        name="in_proj",
    )(x2d, g, w, bg_full, cos_t, sin_t)


def _tn_dot(a, b):
    return lax.dot_general(a, b, (((0,), (0,)), ((), ())), preferred_element_type=F32)


def _diff_attn_kernel(q_ref, k_ref, v_ref, lam_ref, subln_ref, o_ref, buf0, buf1, *,
                      lam_init):
    tq, tk, cc = DIFF_TQ, DIFF_TK, DIFF_CC
    qi = pl.program_id(2)
    q = q_ref[...]
    lane = lax.broadcasted_iota(jnp.int32, q.shape, 1)
    zero = jnp.zeros_like(q)
    qs = jnp.concatenate([jnp.where(lane < HEAD_DIM, q, zero),
                          jnp.where(lane >= HEAD_DIM, q, zero)], axis=0)
    chunks = [slice(c * cc, (c + 1) * cc) for c in range(2 * tq // cc)]
    key = lax.broadcasted_iota(jnp.int32, (tk, cc), 0)
    qry = lax.broadcasted_iota(jnp.int32, (tk, cc), 1)

    def scores(cols, k, diag):
        q0 = cols.start % tq
        if diag is not None and q0 + cc <= diag:
            return jnp.full((tk, cc), NEG_BIG, F32), jnp.full((1, cc), NEG_BIG, F32)
        s = _nt_dot(k, qs[cols])
        if diag is not None and q0 < diag + tk - 1:
            s = jnp.where((key + diag) <= (qry + q0), s, NEG_BIG)
        return s, jnp.max(s, axis=0, keepdims=True)

    def accumulate(s, tile_max, m, l, acc, v):
        m_new = jnp.maximum(m, tile_max)
        alpha = jnp.exp2(m - m_new)
        p = jnp.exp2(s - m_new)
        l = alpha * l + jnp.sum(p, axis=0, keepdims=True)
        acc = alpha * acc + _tn_dot(v, p.astype(BF16))
        return m_new, l, acc

    def step(kb_next, kb_pending, diag, buf_in, buf_out, carry):
        m, l, acc, tile_max = carry
        k = k_ref[pl.ds(pl.multiple_of(kb_next * tk, tk), tk), :]
        v = v_ref[pl.ds(pl.multiple_of(kb_pending * tk, tk), tk), :]
        m_n, l_n, acc_n, tile_max_n = [], [], [], []
        for c, cols in enumerate(chunks):
            s_next, mx = scores(cols, k, diag)
            buf_out[:, cols] = s_next
            mc, lc, ac = accumulate(buf_in[:, cols], tile_max[c], m[c], l[c], acc[c], v)
            m_n.append(mc)
            l_n.append(lc)
            acc_n.append(ac)
            tile_max_n.append(mx)
        return m_n, l_n, acc_n, tile_max_n

    per_q = tq // tk
    assert per_q == 2
    d0, d1 = qi * per_q, qi * per_q + 1
    kd0 = k_ref[pl.ds(pl.multiple_of(d0 * tk, tk), tk), :]
    tile_max = []
    for cols in chunks:
        s, mx = scores(cols, kd0, 0)
        buf0[:, cols] = s
        tile_max.append(mx)
    n_c = len(chunks)
    carry = ([jnp.full((1, cc), NEG_BIG, F32)] * n_c, [jnp.zeros((1, cc), F32)] * n_c,
             [jnp.zeros((LANES, cc), F32)] * n_c, tile_max)
    carry = step(d1, d0, tk, buf0, buf1, carry)

    def pair(j, carry):
        kb = d0 - 1 - 2 * j
        carry = step(kb, jnp.where(j == 0, d1, kb + 1), None, buf1, buf0, carry)
        return step(kb - 1, kb, None, buf0, buf1, carry)

    m, l, acc, tile_max = lax.fori_loop(0, qi, pair, carry)
    kb_last = jnp.where(qi == 0, d1, 0)
    v_last = v_ref[pl.ds(pl.multiple_of(kb_last * tk, tk), tk), :]
    outs = [accumulate(buf1[:, cols], tile_max[c], m[c], l[c], acc[c], v_last)
            for c, cols in enumerate(chunks)]
    o = jnp.concatenate([oc[2] / oc[1] for oc in outs], axis=1)

    lp = lam_ref[...]
    e1 = jnp.exp(jnp.sum(lp[0:1, :] * lp[1:2, :], axis=1, keepdims=True))
    e2 = jnp.exp(jnp.sum(lp[2:3, :] * lp[3:4, :], axis=1, keepdims=True))
    lam = e1 - e2 + lam_init
    o = o[:, :tq] - lam * o[:, tq:]
    ms = jnp.mean(o * o, axis=0, keepdims=True)
    o = o * lax.rsqrt(ms + NORM_EPS) * subln_ref[...] * (1.0 - lam_init)
    o_ref[...] = o.T.astype(BF16)


def _diff_attn(proj, lam_p, subln, lam_init):
    b, s, _ = proj.shape
    nq = s // DIFF_TQ
    kcol0 = DIFF_WIDTH // LANES
    vcol0 = 2 * DIFF_WIDTH // LANES
    return pl.pallas_call(
        functools.partial(_diff_attn_kernel, lam_init=lam_init),
        grid=(b, DIFF_HEADS, nq),
        in_specs=[
            pl.BlockSpec((None, DIFF_TQ, LANES), lambda bi, h, qi: (bi, qi, h)),
            pl.BlockSpec((None, s, LANES), lambda bi, h, qi: (bi, 0, kcol0 + h)),
            pl.BlockSpec((None, s, LANES), lambda bi, h, qi: (bi, 0, vcol0 + h)),
            pl.BlockSpec((4, HEAD_DIM), lambda bi, h, qi: (0, 0)),
            pl.BlockSpec((LANES, 1), lambda bi, h, qi: (0, 0)),
        ],
        out_specs=pl.BlockSpec((None, DIFF_TQ, LANES), lambda bi, h, qi: (bi, qi, h)),
        out_shape=jax.ShapeDtypeStruct((b, s, DIFF_WIDTH), BF16),
        scratch_shapes=[pltpu.VMEM((DIFF_TK, 2 * DIFF_TQ), F32)] * 2,
        compiler_params=pltpu.CompilerParams(
            dimension_semantics=("arbitrary", "arbitrary", "arbitrary"),
            vmem_limit_bytes=VMEM_LIMIT),
        name="diff_attn",
    )(proj, proj, proj, lam_p, subln)


def _sb_attn_kernel(q_ref, k_ref, v_ref, o_ref, buf0, buf1):
    tq, tk, cc = SB_TQ, SB_TK, SB_CC
    qi = pl.program_id(2)
    q = q_ref[...]
    lane = lax.broadcasted_iota(jnp.int32, q.shape, 1)
    zero = jnp.zeros_like(q)
    per_q = tq // tk
    qs = jnp.concatenate([jnp.where(lane < HEAD_DIM, q, zero),
                          jnp.where(lane >= HEAD_DIM, q, zero)], axis=0)
    ss = lax.broadcasted_iota(jnp.int32, (tk, 2 * tk), 0)
    jj = lax.broadcasted_iota(jnp.int32, (tk, 2 * tk), 1) % tk
    neg_suffix = jnp.where(jj >= ss, -1.0, 0.0).astype(BF16)
    sign_bit = jnp.uint32(0x80000000)
    chunks = [slice(c * cc, (c + 1) * cc) for c in range(2 * tq // cc)]
    key = lax.broadcasted_iota(jnp.int32, (tk, cc), 0)
    qry = lax.broadcasted_iota(jnp.int32, (tk, cc), 1)

    def scores(cols, k, diag):
        q0 = cols.start % tq
        if diag is not None and q0 + cc <= diag + 1:
            return jnp.full((tk, cc), NEG_BIG, F32), jnp.zeros((1, cc), F32)
        if diag is not None and q0 >= diag + tk:
            diag = None
        z = _nt_dot(k, qs[cols])
        neg_abs = lax.bitcast_convert_type(
            lax.bitcast_convert_type(z, jnp.uint32) | sign_bit, F32)
        sp = jnp.maximum(z, 0.0) + jnp.log2(1.0 + jnp.exp2(neg_abs))
        if diag is not None:
            mask = (key + diag) < (qry + q0)
            sp = jnp.where(mask, sp, 0.0)
        hi = sp.astype(BF16)
        lo = (sp - hi.astype(F32)).astype(BF16)
        arg = z + jnp.dot(neg_suffix, jnp.concatenate([hi, lo], axis=0),
                          preferred_element_type=F32)
        if diag is not None:
            arg = jnp.where(mask, arg, NEG_BIG)
        return arg, jnp.sum(sp, axis=0, keepdims=True)

    def accumulate(arg, rem, acc, v):
        a = jnp.exp2(arg + rem)
        return acc + _tn_dot(v, a.astype(BF16))

    def step(kb_next, diag, buf_in, buf_out, carry):
        rem, acc, key_sum = carry
        k = k_ref[pl.ds(pl.multiple_of(kb_next * tk, tk), tk), :]
        v = v_ref[pl.ds(pl.multiple_of((kb_next + 1) * tk, tk), tk), :]
        rem_n, acc_n, key_sum_n = [], [], []
        for c, cols in enumerate(chunks):
            arg_next, ks = scores(cols, k, diag)
            buf_out[:, cols] = arg_next
            acc_n.append(accumulate(buf_in[:, cols], rem[c], acc[c], v))
            rem_n.append(rem[c] - key_sum[c])
            key_sum_n.append(ks)
        return rem_n, acc_n, key_sum_n

    assert per_q == 2
    d1, d0 = qi * per_q + 1, qi * per_q
    k1 = k_ref[pl.ds(pl.multiple_of(d1 * tk, tk), tk), :]
    key_sum = []
    for cols in chunks:
        arg, ks = scores(cols, k1, tk)
        buf1[:, cols] = arg
        key_sum.append(ks)
    carry = ([jnp.zeros((1, cc), F32)] * len(chunks),
             [jnp.zeros((LANES, cc), F32)] * len(chunks), key_sum)
    carry = step(d0, 0, buf1, buf0, carry)

    def pair(j, carry):
        kb = d0 - 1 - 2 * j
        carry = step(kb, None, buf0, buf1, carry)
        return step(kb - 1, None, buf1, buf0, carry)

    rem, acc, _ = lax.fori_loop(0, qi, pair, carry)
    v0 = v_ref[pl.ds(0, tk), :]
    acc = jnp.concatenate([accumulate(buf0[:, cols], rem[c], acc[c], v0)
                           for c, cols in enumerate(chunks)], axis=1)
    feat = lax.broadcasted_iota(jnp.int32, (LANES, tq), 0)
    o_ref[...] = jnp.where(feat < HEAD_DIM, acc[:, :tq], acc[:, tq:]).T.astype(BF16)


def _sb_attn(proj):
    b, s, _ = proj.shape
    nq = s // SB_TQ
    qcol0 = 3 * DIFF_WIDTH // LANES
    kcol0 = qcol0 + SB_PAIRS
    vcol0 = kcol0 + SB_PAIRS
    return pl.pallas_call(
        _sb_attn_kernel,
        grid=(b, SB_PAIRS, nq),
        in_specs=[
            pl.BlockSpec((None, SB_TQ, LANES), lambda bi, p, qi: (bi, qi, qcol0 + p)),
            pl.BlockSpec((None, s, LANES), lambda bi, p, qi: (bi, 0, kcol0 + p)),
            pl.BlockSpec((None, s, LANES), lambda bi, p, qi: (bi, 0, vcol0 + p)),
        ],
        out_specs=pl.BlockSpec((None, SB_TQ, LANES), lambda bi, p, qi: (bi, qi, p)),
        out_shape=jax.ShapeDtypeStruct((b, s, SB_WIDTH), BF16),
        scratch_shapes=[pltpu.VMEM((SB_TK, 2 * SB_TQ), F32)] * 2,
        compiler_params=pltpu.CompilerParams(
            dimension_semantics=("arbitrary", "arbitrary", "arbitrary"),
            vmem_limit_bytes=VMEM_LIMIT),
        name="sb_attn",
    )(proj, proj, proj)


def _merge_kernel(x_ref, od_ref, os_ref, gd_ref, gs_ref, wd_ref, ws_ref, wo_ref, o_ref):
    pd = jnp.dot(od_ref[...], wd_ref[...], preferred_element_type=F32)
    ps = jnp.dot(os_ref[...], ws_ref[...], preferred_element_type=F32)
    merged = gd_ref[...].astype(F32) * pd + gs_ref[...].astype(F32) * ps
    o_ref[...] = x_ref[...] + jnp.dot(merged.astype(BF16), wo_ref[...],
                                      preferred_element_type=F32)


def _merge(x2d, o_diff, o_sb, proj2d, w_o_diff, w_o_sb, w_out):
    t_rows = x2d.shape[0]
    tm = MERGE_TM
    gd_blk = GATE_COL0 // D_MODEL
    const = lambda i: (0, 0)
    return pl.pallas_call(
        _merge_kernel,
        grid=(t_rows // tm,),
        in_specs=[
            pl.BlockSpec((tm, D_MODEL), lambda i: (i, 0)),
            pl.BlockSpec((tm, DIFF_WIDTH), lambda i: (i, 0)),
            pl.BlockSpec((tm, SB_WIDTH), lambda i: (i, 0)),
            pl.BlockSpec((tm, D_MODEL), lambda i: (i, gd_blk)),
            pl.BlockSpec((tm, D_MODEL), lambda i: (i, gd_blk + 1)),
            pl.BlockSpec((DIFF_WIDTH, D_MODEL), const),
            pl.BlockSpec((SB_WIDTH, D_MODEL), const),
            pl.BlockSpec((D_MODEL, D_MODEL), const),
        ],
        out_specs=pl.BlockSpec((tm, D_MODEL), lambda i: (i, 0)),
        out_shape=jax.ShapeDtypeStruct((t_rows, D_MODEL), F32),
        compiler_params=pltpu.CompilerParams(
            dimension_semantics=("arbitrary",), vmem_limit_bytes=VMEM_LIMIT),
        name="merge",
    )(x2d, o_diff, o_sb, proj2d, proj2d, w_o_diff, w_o_sb, w_out)


def _ffn_kernel(x_ref, g_ref, wg_ref, wu_ref, wd_ref, gf_ref, o_ref, h_ref, acc_ref,
                *, final_norm):
    c = pl.program_id(1)

    @pl.when(c == 0)
    def _():
        x = x_ref[...]
        ms = jnp.mean(x * x, axis=-1, keepdims=True)
        h_ref[...] = (x * lax.rsqrt(ms + NORM_EPS) * g_ref[...]).astype(BF16)
        acc_ref[...] = x

    h = h_ref[...]
    gate = jnp.dot(h, wg_ref[...], preferred_element_type=F32)
    up = jnp.dot(h, wu_ref[...], preferred_element_type=F32)
    act = (gate * jax.nn.sigmoid(gate) * up).astype(BF16)
    acc_ref[...] += jnp.dot(act, wd_ref[...], preferred_element_type=F32)

    @pl.when(c == pl.num_programs(1) - 1)
    def _():
        y = acc_ref[...]
        if final_norm:
            ms = jnp.mean(y * y, axis=-1, keepdims=True)
            y = y * lax.rsqrt(ms + NORM_EPS) * gf_ref[...]
        o_ref[...] = y


def _ffn(x2d, g, w_in, w_out, g_final, final_norm):
    t_rows = x2d.shape[0]
    tm, th = FFN_TM, FFN_TH
    n_c = FFN_HIDDEN // th
    return pl.pallas_call(
        functools.partial(_ffn_kernel, final_norm=final_norm),
        grid=(t_rows // tm, n_c),
        in_specs=[
            pl.BlockSpec((tm, D_MODEL), lambda i, c: (i, 0)),
            pl.BlockSpec((1, D_MODEL), lambda i, c: (0, 0)),
            pl.BlockSpec((D_MODEL, th), lambda i, c: (0, c)),
            pl.BlockSpec((D_MODEL, th), lambda i, c: (0, n_c + c)),
            pl.BlockSpec((th, D_MODEL), lambda i, c: (c, 0)),
            pl.BlockSpec((1, D_MODEL), lambda i, c: (0, 0)),
        ],
        out_specs=pl.BlockSpec((tm, D_MODEL), lambda i, c: (i, 0)),
        out_shape=jax.ShapeDtypeStruct((t_rows, D_MODEL), F32),
        scratch_shapes=[pltpu.VMEM((tm, D_MODEL), BF16), pltpu.VMEM((tm, D_MODEL), F32)],
        compiler_params=pltpu.CompilerParams(
            dimension_semantics=("arbitrary", "arbitrary"), vmem_limit_bytes=VMEM_LIMIT),
        name="ffn",
    )(x2d, g, w_in, w_in, w_out, g_final)


def _rope_tables(seq):
    half = HEAD_DIM // 2
    pos = jnp.arange(seq, dtype=F32)
    inv = ROPE_THETA ** (-jnp.arange(0, HEAD_DIM, 2, dtype=F32) / HEAD_DIM)
    ang = pos[:, None] * inv[None, :]
    cos, sin = jnp.cos(ang), jnp.sin(ang)
    cos_t = jnp.tile(cos, (1, LANES // half))
    sin_t = jnp.tile(jnp.concatenate([-sin, sin], axis=1), (1, LANES // HEAD_DIM))
    return cos_t, sin_t


def kernel(x, norm_attn, w_in, b_gate, diff_lambda, diff_subln, w_o_diff, w_o_sb, w_out,
           norm_ffn, w_ffn_in, w_ffn_out, norm_final):
    b, s, d = x.shape
    cos_t, sin_t = _rope_tables(s)
    x2d = x.reshape(b * s, d)
    g_final = norm_final.reshape(1, d)
    for layer in range(DEPTH):
        lam_init = 0.8 - 0.6 * math.exp(-0.3 * layer)
        bg_full = jnp.concatenate(
            [jnp.zeros((GATE_COL0,), F32), b_gate[layer]]).reshape(1, IN_COLS)
        proj2d = _in_proj(x2d, norm_attn[layer].reshape(1, d), w_in[layer].astype(BF16),
                          bg_full, cos_t, sin_t, s)
        proj = proj2d.reshape(b, s, IN_COLS)
        o_diff = _diff_attn(proj, diff_lambda[layer], diff_subln[layer].reshape(LANES, 1),
                            lam_init)
        o_sb = _sb_attn(proj)
        x2d = _merge(x2d, o_diff.reshape(b * s, DIFF_WIDTH), o_sb.reshape(b * s, SB_WIDTH),
                     proj2d, w_o_diff[layer].astype(BF16), w_o_sb[layer].astype(BF16),
                     w_out[layer].astype(BF16))
        x2d = _ffn(x2d, norm_ffn[layer].reshape(1, d), w_ffn_in[layer].astype(BF16),
                   w_ffn_out[layer].astype(BF16), g_final, layer == DEPTH - 1)
    return x2d.reshape(b, s, d)
```

```python
import functools
import math

import jax
import jax.numpy as jnp
from jax import lax
from jax.experimental import pallas as pl
from jax.experimental.pallas import tpu as pltpu

D_MODEL = 1024
DEPTH = 2
HEAD_DIM = 64
LANES = 128
DIFF_HEADS = 4
DIFF_WIDTH = DIFF_HEADS * 2 * HEAD_DIM
SB_HEADS = 8
SB_WIDTH = SB_HEADS * HEAD_DIM
SB_PAIRS = SB_WIDTH // LANES
IN_COLS = 3 * DIFF_WIDTH + 3 * SB_WIDTH + 2 * D_MODEL
GATE_COL0 = 3 * DIFF_WIDTH + 3 * SB_WIDTH
FFN_HIDDEN = 2816
ROPE_THETA = 10000.0
NORM_EPS = 1e-6
NEG_BIG = -1e30
LOG2E = math.log2(math.e)
SB_DEAD_LOG2 = -160.0

VMEM_LIMIT = 48 * 1024 * 1024

IN_TM, IN_TN = 1024, 512
DIFF_TQ, DIFF_TK, DIFF_CC = 512, 256, 256
SB_TQ, SB_TK, SB_CC = 512, 256, 256
MERGE_TM = 512
FFN_TM, FFN_TH = 1024, 1408

BF16 = jnp.bfloat16
F32 = jnp.float32


def _nt_dot(a, b):
    return lax.dot_general(a, b, (((1,), (1,)), ((), ())), preferred_element_type=F32)


def _in_proj_kernel(x_ref, g_ref, w_ref, bg_ref, cos_ref, sin_ref, o_ref, h_ref):
    j = pl.program_id(1)

    @pl.when(j == 0)
    def _():
        x = x_ref[...]
        ms = jnp.mean(x * x, axis=-1, keepdims=True)
        h_ref[...] = (x * lax.rsqrt(ms + NORM_EPS) * g_ref[...]).astype(BF16)

    acc = jnp.dot(h_ref[...], w_ref[...], preferred_element_type=F32)

    n_diff_q = DIFF_WIDTH // IN_TN
    n_rope = 2 * DIFF_WIDTH // IN_TN
    sb_q0 = 3 * DIFF_WIDTH // IN_TN
    sb_q1 = (3 * DIFF_WIDTH + SB_WIDTH) // IN_TN
    gate0 = GATE_COL0 // IN_TN
    scale = HEAD_DIM ** -0.5 * LOG2E

    @pl.when(j < n_rope)
    def _():
        cos = cos_ref[...]
        sin = sin_ref[...]
        lane = lax.broadcasted_iota(jnp.int32, cos.shape, 1)
        first_half = (lane % HEAD_DIM) < (HEAD_DIM // 2)
        sc = jnp.where(j < n_diff_q, scale, 1.0).astype(F32)
        for c in range(IN_TN // LANES):
            t = acc[:, c * LANES:(c + 1) * LANES]
            partner = jnp.where(first_half,
                                pltpu.roll(t, LANES - HEAD_DIM // 2, 1),
                                pltpu.roll(t, HEAD_DIM // 2, 1))
            r = (t * cos + partner * sin) * sc
            o_ref[:, c * LANES:(c + 1) * LANES] = r.astype(BF16)

    @pl.when((j >= n_rope) & (j < gate0))
    def _():
        sc = jnp.where((j >= sb_q0) & (j < sb_q1), scale, 1.0).astype(F32)
        o_ref[...] = (acc * sc).astype(BF16)

    @pl.when(j >= gate0)
    def _():
        o_ref[...] = jax.nn.sigmoid(acc + bg_ref[...]).astype(BF16)


def _in_proj(x2d, g, w, bg_full, cos_t, sin_t, seq):
    t_rows = x2d.shape[0]
    n_i, n_j = t_rows // IN_TM, IN_COLS // IN_TN
    pos_tiles = seq // IN_TM
    return pl.pallas_call(
        _in_proj_kernel,
        grid=(n_i, n_j),
        in_specs=[
            pl.BlockSpec((IN_TM, D_MODEL), lambda i, j: (i, 0)),
            pl.BlockSpec((1, D_MODEL), lambda i, j: (0, 0)),
            pl.BlockSpec((D_MODEL, IN_TN), lambda i, j: (0, j)),
            pl.BlockSpec((1, IN_TN), lambda i, j: (0, j)),
            pl.BlockSpec((IN_TM, LANES), lambda i, j: (i % pos_tiles, 0)),
            pl.BlockSpec((IN_TM, LANES), lambda i, j: (i % pos_tiles, 0)),
        ],
        out_specs=pl.BlockSpec((IN_TM, IN_TN), lambda i, j: (i, j)),
        out_shape=jax.ShapeDtypeStruct((t_rows, IN_COLS), BF16),
        scratch_shapes=[pltpu.VMEM((IN_TM, D_MODEL), BF16)],
        compiler_params=pltpu.CompilerParams(
            dimension_semantics=("arbitrary", "arbitrary"), vmem_limit_bytes=VMEM_LIMIT),
        name="in_proj",
    )(x2d, g, w, bg_full, cos_t, sin_t)


def _tn_dot(a, b):
    return lax.dot_general(a, b, (((0,), (0,)), ((), ())), preferred_element_type=F32)


def _diff_attn_kernel(q_ref, k_ref, v_ref, lam_ref, subln_ref, o_ref, buf0, buf1, *,
                      lam_init):
    tq, tk, cc = DIFF_TQ, DIFF_TK, DIFF_CC
    qi = pl.program_id(2)
    q = q_ref[...]
    lane = lax.broadcasted_iota(jnp.int32, q.shape, 1)
    zero = jnp.zeros_like(q)
    qs = jnp.concatenate([jnp.where(lane < HEAD_DIM, q, zero),
                          jnp.where(lane >= HEAD_DIM, q, zero)], axis=0)
    chunks = [slice(c * cc, (c + 1) * cc) for c in range(2 * tq // cc)]
    key = lax.broadcasted_iota(jnp.int32, (tk, cc), 0)
    qry = lax.broadcasted_iota(jnp.int32, (tk, cc), 1)

    def scores(cols, k, diag):
        q0 = cols.start % tq
        if diag is not None and q0 + cc <= diag:
            return jnp.full((tk, cc), NEG_BIG, F32), jnp.full((1, cc), NEG_BIG, F32)
        s = _nt_dot(k, qs[cols])
        if diag is not None and q0 < diag + tk - 1:
            s = jnp.where((key + diag) <= (qry + q0), s, NEG_BIG)
        return s, jnp.max(s, axis=0, keepdims=True)

    def accumulate(s, tile_max, m, l, acc, v):
        m_new = jnp.maximum(m, tile_max)
        alpha = jnp.exp2(m - m_new)
        p = jnp.exp2(s - m_new)
        l = alpha * l + jnp.sum(p, axis=0, keepdims=True)
        acc = alpha * acc + _tn_dot(v, p.astype(BF16))
        return m_new, l, acc

    def step(kb_next, kb_pending, diag, buf_in, buf_out, carry):
        m, l, acc, tile_max = carry
        k = k_ref[pl.ds(pl.multiple_of(kb_next * tk, tk), tk), :]
        v = v_ref[pl.ds(pl.multiple_of(kb_pending * tk, tk), tk), :]
        m_n, l_n, acc_n, tile_max_n = [], [], [], []
        for c, cols in enumerate(chunks):
            s_next, mx = scores(cols, k, diag)
            buf_out[:, cols] = s_next
            mc, lc, ac = accumulate(buf_in[:, cols], tile_max[c], m[c], l[c], acc[c], v)
            m_n.append(mc)
            l_n.append(lc)
            acc_n.append(ac)
            tile_max_n.append(mx)
        return m_n, l_n, acc_n, tile_max_n

    per_q = tq // tk
    assert per_q == 2
    d0, d1 = qi * per_q, qi * per_q + 1
    kd0 = k_ref[pl.ds(pl.multiple_of(d0 * tk, tk), tk), :]
    tile_max = []
    for cols in chunks:
        s, mx = scores(cols, kd0, 0)
        buf0[:, cols] = s
        tile_max.append(mx)
    n_c = len(chunks)
    carry = ([jnp.full((1, cc), NEG_BIG, F32)] * n_c, [jnp.zeros((1, cc), F32)] * n_c,
             [jnp.zeros((LANES, cc), F32)] * n_c, tile_max)
    carry = step(d1, d0, tk, buf0, buf1, carry)

    def pair(j, carry):
        kb = d0 - 1 - 2 * j
        carry = step(kb, jnp.where(j == 0, d1, kb + 1), None, buf1, buf0, carry)
        return step(kb - 1, kb, None, buf0, buf1, carry)

    m, l, acc, tile_max = lax.fori_loop(0, qi, pair, carry)
    kb_last = jnp.where(qi == 0, d1, 0)
    v_last = v_ref[pl.ds(pl.multiple_of(kb_last * tk, tk), tk), :]
    outs = [accumulate(buf1[:, cols], tile_max[c], m[c], l[c], acc[c], v_last)
            for c, cols in enumerate(chunks)]
    o = jnp.concatenate([oc[2] / oc[1] for oc in outs], axis=1)

    lp = lam_ref[...]
    e1 = jnp.exp(jnp.sum(lp[0:1, :] * lp[1:2, :], axis=1, keepdims=True))
    e2 = jnp.exp(jnp.sum(lp[2:3, :] * lp[3:4, :], axis=1, keepdims=True))
    lam = e1 - e2 + lam_init
    o = o[:, :tq] - lam * o[:, tq:]
    ms = jnp.mean(o * o, axis=0, keepdims=True)
    o = o * lax.rsqrt(ms + NORM_EPS) * subln_ref[...] * (1.0 - lam_init)
    o_ref[...] = o.T.astype(BF16)


def _diff_attn(proj, lam_p, subln, lam_init):
    b, s, _ = proj.shape
    nq = s // DIFF_TQ
    kcol0 = DIFF_WIDTH // LANES
    vcol0 = 2 * DIFF_WIDTH // LANES
    return pl.pallas_call(
        functools.partial(_diff_attn_kernel, lam_init=lam_init),
        grid=(b, DIFF_HEADS, nq),
        in_specs=[
            pl.BlockSpec((None, DIFF_TQ, LANES), lambda bi, h, qi: (bi, qi, h)),
            pl.BlockSpec((None, s, LANES), lambda bi, h, qi: (bi, 0, kcol0 + h)),
            pl.BlockSpec((None, s, LANES), lambda bi, h, qi: (bi, 0, vcol0 + h)),
            pl.BlockSpec((4, HEAD_DIM), lambda bi, h, qi: (0, 0)),
            pl.BlockSpec((LANES, 1), lambda bi, h, qi: (0, 0)),
        ],
        out_specs=pl.BlockSpec((None, DIFF_TQ, LANES), lambda bi, h, qi: (bi, qi, h)),
        out_shape=jax.ShapeDtypeStruct((b, s, DIFF_WIDTH), BF16),
        scratch_shapes=[pltpu.VMEM((DIFF_TK, 2 * DIFF_TQ), F32)] * 2,
        compiler_params=pltpu.CompilerParams(
            dimension_semantics=("arbitrary", "arbitrary", "arbitrary"),
            vmem_limit_bytes=VMEM_LIMIT),
        name="diff_attn",
    )(proj, proj, proj, lam_p, subln)


def _sb_attn_kernel(q_ref, k_ref, v_ref, o_ref, buf0, buf1):
    tq, tk, cc = SB_TQ, SB_TK, SB_CC
    qi = pl.program_id(2)
    q = q_ref[...]
    lane = lax.broadcasted_iota(jnp.int32, q.shape, 1)
    zero = jnp.zeros_like(q)
    per_q = tq // tk
    qs = jnp.concatenate([jnp.where(lane < HEAD_DIM, q, zero),
                          jnp.where(lane >= HEAD_DIM, q, zero)], axis=0)
    ss = lax.broadcasted_iota(jnp.int32, (tk, 2 * tk), 0)
    jj = lax.broadcasted_iota(jnp.int32, (tk, 2 * tk), 1) % tk
    neg_suffix = jnp.where(jj >= ss, -1.0, 0.0).astype(BF16)
    sign_bit = jnp.uint32(0x80000000)
    chunks = [slice(c * cc, (c + 1) * cc) for c in range(2 * tq // cc)]
    key = lax.broadcasted_iota(jnp.int32, (tk, cc), 0)
    qry = lax.broadcasted_iota(jnp.int32, (tk, cc), 1)

    def scores(cols, k, diag):
        q0 = cols.start % tq
        if diag is not None and q0 + cc <= diag + 1:
            return jnp.full((tk, cc), NEG_BIG, F32), jnp.zeros((1, cc), F32)
        if diag is not None and q0 >= diag + tk:
            diag = None
        z = _nt_dot(k, qs[cols])
        neg_abs = lax.bitcast_convert_type(
            lax.bitcast_convert_type(z, jnp.uint32) | sign_bit, F32)
        sp = jnp.maximum(z, 0.0) + jnp.log2(1.0 + jnp.exp2(neg_abs))
        if diag is not None:
            mask = (key + diag) < (qry + q0)
            sp = jnp.where(mask, sp, 0.0)
        hi = sp.astype(BF16)
        lo = (sp - hi.astype(F32)).astype(BF16)
        arg = z + jnp.dot(neg_suffix, jnp.concatenate([hi, lo], axis=0),
                          preferred_element_type=F32)
        if diag is not None:
            arg = jnp.where(mask, arg, NEG_BIG)
        return arg, jnp.sum(sp, axis=0, keepdims=True)

    def accumulate(arg, rem, acc, v):
        a = jnp.exp2(arg + rem)
        return acc + _tn_dot(v, a.astype(BF16))

    def step(kb_next, diag, buf_in, buf_out, carry):
        rem, acc, key_sum = carry
        k = k_ref[pl.ds(pl.multiple_of(kb_next * tk, tk), tk), :]
        v = v_ref[pl.ds(pl.multiple_of((kb_next + 1) * tk, tk), tk), :]
        rem_n, acc_n, key_sum_n = [], [], []
        for c, cols in enumerate(chunks):
            arg_next, ks = scores(cols, k, diag)
            buf_out[:, cols] = arg_next
            acc_n.append(accumulate(buf_in[:, cols], rem[c], acc[c], v))
            rem_n.append(rem[c] - key_sum[c])
            key_sum_n.append(ks)
        return rem_n, acc_n, key_sum_n

    assert per_q == 2
    d1, d0 = qi * per_q + 1, qi * per_q
    k1 = k_ref[pl.ds(pl.multiple_of(d1 * tk, tk), tk), :]
    key_sum = []
    for cols in chunks:
        arg, ks = scores(cols, k1, tk)
        buf1[:, cols] = arg
        key_sum.append(ks)
    carry = ([jnp.zeros((1, cc), F32)] * len(chunks),
             [jnp.zeros((LANES, cc), F32)] * len(chunks), key_sum)
    carry = step(d0, 0, buf1, buf0, carry)

    def live(rem):
        top = functools.reduce(jnp.maximum, rem)
        return (jnp.max(top) > SB_DEAD_LOG2).astype(jnp.int32)

    def pair(state):
        j, _, carry = state
        kb = d0 - 1 - 2 * j
        carry = step(kb, None, buf0, buf1, carry)
        carry = step(kb - 1, None, buf1, buf0, carry)
        return j + 1, live(carry[0]), carry

    n_pairs, _, (rem, acc, _) = lax.while_loop(
        lambda st: (st[0] < qi) & (st[1] > 0), pair, (jnp.int32(0), live(carry[0]), carry))
    pending = d0 - 2 * n_pairs
    v_last = v_ref[pl.ds(pl.multiple_of(pending * tk, tk), tk), :]
    acc = jnp.concatenate([accumulate(buf0[:, cols], rem[c], acc[c], v_last)
                           for c, cols in enumerate(chunks)], axis=1)
    feat = lax.broadcasted_iota(jnp.int32, (LANES, tq), 0)
    o_ref[...] = jnp.where(feat < HEAD_DIM, acc[:, :tq], acc[:, tq:]).T.astype(BF16)


def _sb_attn(proj):
    b, s, _ = proj.shape
    nq = s // SB_TQ
    qcol0 = 3 * DIFF_WIDTH // LANES
    kcol0 = qcol0 + SB_PAIRS
    vcol0 = kcol0 + SB_PAIRS
    return pl.pallas_call(
        _sb_attn_kernel,
        grid=(b, SB_PAIRS, nq),
        in_specs=[
            pl.BlockSpec((None, SB_TQ, LANES), lambda bi, p, qi: (bi, qi, qcol0 + p)),
            pl.BlockSpec((None, s, LANES), lambda bi, p, qi: (bi, 0, kcol0 + p)),
            pl.BlockSpec((None, s, LANES), lambda bi, p, qi: (bi, 0, vcol0 + p)),
        ],
        out_specs=pl.BlockSpec((None, SB_TQ, LANES), lambda bi, p, qi: (bi, qi, p)),
        out_shape=jax.ShapeDtypeStruct((b, s, SB_WIDTH), BF16),
        scratch_shapes=[pltpu.VMEM((SB_TK, 2 * SB_TQ), F32)] * 2,
        compiler_params=pltpu.CompilerParams(
            dimension_semantics=("arbitrary", "arbitrary", "arbitrary"),
            vmem_limit_bytes=VMEM_LIMIT),
        name="sb_attn",
    )(proj, proj, proj)


def _merge_kernel(x_ref, od_ref, os_ref, gd_ref, gs_ref, wd_ref, ws_ref, wo_ref, o_ref):
    pd = jnp.dot(od_ref[...], wd_ref[...], preferred_element_type=F32)
    ps = jnp.dot(os_ref[...], ws_ref[...], preferred_element_type=F32)
    merged = gd_ref[...].astype(F32) * pd + gs_ref[...].astype(F32) * ps
    o_ref[...] = x_ref[...] + jnp.dot(merged.astype(BF16), wo_ref[...],
                                      preferred_element_type=F32)


def _merge(x2d, o_diff, o_sb, proj2d, w_o_diff, w_o_sb, w_out):
    t_rows = x2d.shape[0]
    tm = MERGE_TM
    gd_blk = GATE_COL0 // D_MODEL
    const = lambda i: (0, 0)
    return pl.pallas_call(
        _merge_kernel,
        grid=(t_rows // tm,),
        in_specs=[
            pl.BlockSpec((tm, D_MODEL), lambda i: (i, 0)),
            pl.BlockSpec((tm, DIFF_WIDTH), lambda i: (i, 0)),
            pl.BlockSpec((tm, SB_WIDTH), lambda i: (i, 0)),
            pl.BlockSpec((tm, D_MODEL), lambda i: (i, gd_blk)),
            pl.BlockSpec((tm, D_MODEL), lambda i: (i, gd_blk + 1)),
            pl.BlockSpec((DIFF_WIDTH, D_MODEL), const),
            pl.BlockSpec((SB_WIDTH, D_MODEL), const),
            pl.BlockSpec((D_MODEL, D_MODEL), const),
        ],
        out_specs=pl.BlockSpec((tm, D_MODEL), lambda i: (i, 0)),
        out_shape=jax.ShapeDtypeStruct((t_rows, D_MODEL), F32),
        compiler_params=pltpu.CompilerParams(
            dimension_semantics=("arbitrary",), vmem_limit_bytes=VMEM_LIMIT),
        name="merge",
    )(x2d, o_diff, o_sb, proj2d, proj2d, w_o_diff, w_o_sb, w_out)


def _ffn_kernel(x_ref, g_ref, wg_ref, wu_ref, wd_ref, gf_ref, o_ref, h_ref, acc_ref,
                *, final_norm):
    c = pl.program_id(1)

    @pl.when(c == 0)
    def _():
        x = x_ref[...]
        ms = jnp.mean(x * x, axis=-1, keepdims=True)
        h_ref[...] = (x * lax.rsqrt(ms + NORM_EPS) * g_ref[...]).astype(BF16)
        acc_ref[...] = x

    h = h_ref[...]
    gate = jnp.dot(h, wg_ref[...], preferred_element_type=F32)
    up = jnp.dot(h, wu_ref[...], preferred_element_type=F32)
    act = (gate * jax.nn.sigmoid(gate) * up).astype(BF16)
    acc_ref[...] += jnp.dot(act, wd_ref[...], preferred_element_type=F32)

    @pl.when(c == pl.num_programs(1) - 1)
    def _():
        y = acc_ref[...]
        if final_norm:
            ms = jnp.mean(y * y, axis=-1, keepdims=True)
            y = y * lax.rsqrt(ms + NORM_EPS) * gf_ref[...]
        o_ref[...] = y


def _ffn(x2d, g, w_in, w_out, g_final, final_norm):
    t_rows = x2d.shape[0]
    tm, th = FFN_TM, FFN_TH
    n_c = FFN_HIDDEN // th
    return pl.pallas_call(
        functools.partial(_ffn_kernel, final_norm=final_norm),
        grid=(t_rows // tm, n_c),
        in_specs=[
            pl.BlockSpec((tm, D_MODEL), lambda i, c: (i, 0)),
            pl.BlockSpec((1, D_MODEL), lambda i, c: (0, 0)),
            pl.BlockSpec((D_MODEL, th), lambda i, c: (0, c)),
            pl.BlockSpec((D_MODEL, th), lambda i, c: (0, n_c + c)),
            pl.BlockSpec((th, D_MODEL), lambda i, c: (c, 0)),
            pl.BlockSpec((1, D_MODEL), lambda i, c: (0, 0)),
        ],
        out_specs=pl.BlockSpec((tm, D_MODEL), lambda i, c: (i, 0)),
        out_shape=jax.ShapeDtypeStruct((t_rows, D_MODEL), F32),
        scratch_shapes=[pltpu.VMEM((tm, D_MODEL), BF16), pltpu.VMEM((tm, D_MODEL), F32)],
        compiler_params=pltpu.CompilerParams(
            dimension_semantics=("arbitrary", "arbitrary"), vmem_limit_bytes=VMEM_LIMIT),
        name="ffn",
    )(x2d, g, w_in, w_in, w_out, g_final)


def _rope_tables(seq):
    half = HEAD_DIM // 2
    pos = jnp.arange(seq, dtype=F32)
    inv = ROPE_THETA ** (-jnp.arange(0, HEAD_DIM, 2, dtype=F32) / HEAD_DIM)
    ang = pos[:, None] * inv[None, :]
    cos, sin = jnp.cos(ang), jnp.sin(ang)
    cos_t = jnp.tile(cos, (1, LANES // half))
    sin_t = jnp.tile(jnp.concatenate([-sin, sin], axis=1), (1, LANES // HEAD_DIM))
    return cos_t, sin_t


def kernel(x, norm_attn, w_in, b_gate, diff_lambda, diff_subln, w_o_diff, w_o_sb, w_out,
           norm_ffn, w_ffn_in, w_ffn_out, norm_final):
    b, s, d = x.shape
    cos_t, sin_t = _rope_tables(s)
    x2d = x.reshape(b * s, d)
    g_final = norm_final.reshape(1, d)
    for layer in range(DEPTH):
        lam_init = 0.8 - 0.6 * math.exp(-0.3 * layer)
        bg_full = jnp.concatenate(
            [jnp.zeros((GATE_COL0,), F32), b_gate[layer]]).reshape(1, IN_COLS)
        proj2d = _in_proj(x2d, norm_attn[layer].reshape(1, d), w_in[layer].astype(BF16),
                          bg_full, cos_t, sin_t, s)
        proj = proj2d.reshape(b, s, IN_COLS)
        o_diff = _diff_attn(proj, diff_lambda[layer], diff_subln[layer].reshape(LANES, 1),
                            lam_init)
        o_sb = _sb_attn(proj)
        x2d = _merge(x2d, o_diff.reshape(b * s, DIFF_WIDTH), o_sb.reshape(b * s, SB_WIDTH),
                     proj2d, w_o_diff[layer].astype(BF16), w_o_sb[layer].astype(BF16),
                     w_out[layer].astype(BF16))
        x2d = _ffn(x2d, norm_ffn[layer].reshape(1, d), w_ffn_in[layer].astype(BF16),
                   w_ffn_out[layer].astype(BF16), g_final, layer == DEPTH - 1)
    return x2d.reshape(b, s, d)
```

```python
import functools
import math

import jax
import jax.numpy as jnp
from jax import lax
from jax.experimental import pallas as pl
from jax.experimental.pallas import tpu as pltpu

D_MODEL = 1024
DEPTH = 2
HEAD_DIM = 64
LANES = 128
DIFF_HEADS = 4
DIFF_WIDTH = DIFF_HEADS * 2 * HEAD_DIM
SB_HEADS = 8
SB_WIDTH = SB_HEADS * HEAD_DIM
SB_PAIRS = SB_WIDTH // LANES
IN_COLS = 3 * DIFF_WIDTH + 3 * SB_WIDTH + 2 * D_MODEL
GATE_COL0 = 3 * DIFF_WIDTH + 3 * SB_WIDTH
FFN_HIDDEN = 2816
ROPE_THETA = 10000.0
NORM_EPS = 1e-6
NEG_BIG = -1e30
LOG2E = math.log2(math.e)
SB_DEAD_LOG2 = -160.0

VMEM_LIMIT = 48 * 1024 * 1024

IN_TM, IN_TN = 1024, 512
DIFF_TQ, DIFF_TK, DIFF_CC = 1024, 512, 256
SB_TQ, SB_TK, SB_CC = 512, 256, 256
MERGE_TM = 512
FFN_TM, FFN_TH = 1024, 1408

BF16 = jnp.bfloat16
F32 = jnp.float32


def _nt_dot(a, b):
    return lax.dot_general(a, b, (((1,), (1,)), ((), ())), preferred_element_type=F32)


def _in_proj_kernel(x_ref, g_ref, w_ref, bg_ref, cos_ref, sin_ref, o_ref, h_ref):
    j = pl.program_id(1)

    @pl.when(j == 0)
    def _():
        x = x_ref[...]
        ms = jnp.mean(x * x, axis=-1, keepdims=True)
        h_ref[...] = (x * lax.rsqrt(ms + NORM_EPS) * g_ref[...]).astype(BF16)

    acc = jnp.dot(h_ref[...], w_ref[...], preferred_element_type=F32)

    n_diff_q = DIFF_WIDTH // IN_TN
    n_rope = 2 * DIFF_WIDTH // IN_TN
    sb_q0 = 3 * DIFF_WIDTH // IN_TN
    sb_q1 = (3 * DIFF_WIDTH + SB_WIDTH) // IN_TN
    gate0 = GATE_COL0 // IN_TN
    scale = HEAD_DIM ** -0.5 * LOG2E

    @pl.when(j < n_rope)
    def _():
        cos = cos_ref[...]
        sin = sin_ref[...]
        lane = lax.broadcasted_iota(jnp.int32, cos.shape, 1)
        first_half = (lane % HEAD_DIM) < (HEAD_DIM // 2)
        sc = jnp.where(j < n_diff_q, scale, 1.0).astype(F32)
        for c in range(IN_TN // LANES):
            t = acc[:, c * LANES:(c + 1) * LANES]
            partner = jnp.where(first_half,
                                pltpu.roll(t, LANES - HEAD_DIM // 2, 1),
                                pltpu.roll(t, HEAD_DIM // 2, 1))
            r = (t * cos + partner * sin) * sc
            o_ref[:, c * LANES:(c + 1) * LANES] = r.astype(BF16)

    @pl.when((j >= n_rope) & (j < gate0))
    def _():
        sc = jnp.where((j >= sb_q0) & (j < sb_q1), scale, 1.0).astype(F32)
        o_ref[...] = (acc * sc).astype(BF16)

    @pl.when(j >= gate0)
    def _():
        o_ref[...] = jax.nn.sigmoid(acc + bg_ref[...]).astype(BF16)


def _in_proj(x2d, g, w, bg_full, cos_t, sin_t, seq):
    t_rows = x2d.shape[0]
    n_i, n_j = t_rows // IN_TM, IN_COLS // IN_TN
    pos_tiles = seq // IN_TM
    return pl.pallas_call(
        _in_proj_kernel,
        grid=(n_i, n_j),
        in_specs=[
            pl.BlockSpec((IN_TM, D_MODEL), lambda i, j: (i, 0)),
            pl.BlockSpec((1, D_MODEL), lambda i, j: (0, 0)),
            pl.BlockSpec((D_MODEL, IN_TN), lambda i, j: (0, j)),
            pl.BlockSpec((1, IN_TN), lambda i, j: (0, j)),
            pl.BlockSpec((IN_TM, LANES), lambda i, j: (i % pos_tiles, 0)),
            pl.BlockSpec((IN_TM, LANES), lambda i, j: (i % pos_tiles, 0)),
        ],
        out_specs=pl.BlockSpec((IN_TM, IN_TN), lambda i, j: (i, j)),
        out_shape=jax.ShapeDtypeStruct((t_rows, IN_COLS), BF16),
        scratch_shapes=[pltpu.VMEM((IN_TM, D_MODEL), BF16)],
        compiler_params=pltpu.CompilerParams(
            dimension_semantics=("arbitrary", "arbitrary"), vmem_limit_bytes=VMEM_LIMIT),
        name="in_proj",
    )(x2d, g, w, bg_full, cos_t, sin_t)


def _tn_dot(a, b):
    return lax.dot_general(a, b, (((0,), (0,)), ((), ())), preferred_element_type=F32)


def _diff_attn_kernel(q_ref, k_ref, v_ref, lam_ref, subln_ref, o_ref, buf0, buf1, *,
                      lam_init):
    tq, tk, cc = DIFF_TQ, DIFF_TK, DIFF_CC
    qi = pl.program_id(2)
    q = q_ref[...]
    lane = lax.broadcasted_iota(jnp.int32, q.shape, 1)
    zero = jnp.zeros_like(q)
    qs = jnp.concatenate([jnp.where(lane < HEAD_DIM, q, zero),
                          jnp.where(lane >= HEAD_DIM, q, zero)], axis=0)
    chunks = [slice(c * cc, (c + 1) * cc) for c in range(2 * tq // cc)]
    key = lax.broadcasted_iota(jnp.int32, (tk, cc), 0)
    qry = lax.broadcasted_iota(jnp.int32, (tk, cc), 1)

    def scores(cols, k, diag):
        q0 = cols.start % tq
        if diag is not None and q0 + cc <= diag:
            return jnp.full((tk, cc), NEG_BIG, F32), jnp.full((1, cc), NEG_BIG, F32)
        s = _nt_dot(k, qs[cols])
        if diag is not None and q0 < diag + tk - 1:
            s = jnp.where((key + diag) <= (qry + q0), s, NEG_BIG)
        return s, jnp.max(s, axis=0, keepdims=True)

    def accumulate(s, tile_max, m, l, acc, v):
        m_new = jnp.maximum(m, tile_max)
        alpha = jnp.exp2(m - m_new)
        p = jnp.exp2(s - m_new)
        l = alpha * l + jnp.sum(p, axis=0, keepdims=True)
        acc = alpha * acc + _tn_dot(v, p.astype(BF16))
        return m_new, l, acc

    def step(kb_next, kb_pending, diag, buf_in, buf_out, carry):
        m, l, acc, tile_max = carry
        k = k_ref[pl.ds(pl.multiple_of(kb_next * tk, tk), tk), :]
        v = v_ref[pl.ds(pl.multiple_of(kb_pending * tk, tk), tk), :]
        m_n, l_n, acc_n, tile_max_n = [], [], [], []
        for c, cols in enumerate(chunks):
            s_next, mx = scores(cols, k, diag)
            buf_out[:, cols] = s_next
            mc, lc, ac = accumulate(buf_in[:, cols], tile_max[c], m[c], l[c], acc[c], v)
            m_n.append(mc)
            l_n.append(lc)
            acc_n.append(ac)
            tile_max_n.append(mx)
        return m_n, l_n, acc_n, tile_max_n

    per_q = tq // tk
    assert per_q == 2
    d0, d1 = qi * per_q, qi * per_q + 1
    kd0 = k_ref[pl.ds(pl.multiple_of(d0 * tk, tk), tk), :]
    tile_max = []
    for cols in chunks:
        s, mx = scores(cols, kd0, 0)
        buf0[:, cols] = s
        tile_max.append(mx)
    n_c = len(chunks)
    carry = ([jnp.full((1, cc), NEG_BIG, F32)] * n_c, [jnp.zeros((1, cc), F32)] * n_c,
             [jnp.zeros((LANES, cc), F32)] * n_c, tile_max)
    carry = step(d1, d0, tk, buf0, buf1, carry)

    def pair(j, carry):
        kb = d0 - 1 - 2 * j
        carry = step(kb, jnp.where(j == 0, d1, kb + 1), None, buf1, buf0, carry)
        return step(kb - 1, kb, None, buf0, buf1, carry)

    m, l, acc, tile_max = lax.fori_loop(0, qi, pair, carry)
    kb_last = jnp.where(qi == 0, d1, 0)
    v_last = v_ref[pl.ds(pl.multiple_of(kb_last * tk, tk), tk), :]
    outs = [accumulate(buf1[:, cols], tile_max[c], m[c], l[c], acc[c], v_last)
            for c, cols in enumerate(chunks)]
    o = jnp.concatenate([oc[2] / oc[1] for oc in outs], axis=1)

    lp = lam_ref[...]
    e1 = jnp.exp(jnp.sum(lp[0:1, :] * lp[1:2, :], axis=1, keepdims=True))
    e2 = jnp.exp(jnp.sum(lp[2:3, :] * lp[3:4, :], axis=1, keepdims=True))
    lam = e1 - e2 + lam_init
    o = o[:, :tq] - lam * o[:, tq:]
    ms = jnp.mean(o * o, axis=0, keepdims=True)
    o = o * lax.rsqrt(ms + NORM_EPS) * subln_ref[...] * (1.0 - lam_init)
    o_ref[...] = o.T.astype(BF16)


def _diff_attn(proj, lam_p, subln, lam_init):
    b, s, _ = proj.shape
    nq = s // DIFF_TQ
    kcol0 = DIFF_WIDTH // LANES
    vcol0 = 2 * DIFF_WIDTH // LANES
    return pl.pallas_call(
        functools.partial(_diff_attn_kernel, lam_init=lam_init),
        grid=(b, DIFF_HEADS, nq),
        in_specs=[
            pl.BlockSpec((None, DIFF_TQ, LANES), lambda bi, h, qi: (bi, qi, h)),
            pl.BlockSpec((None, s, LANES), lambda bi, h, qi: (bi, 0, kcol0 + h)),
            pl.BlockSpec((None, s, LANES), lambda bi, h, qi: (bi, 0, vcol0 + h)),
            pl.BlockSpec((4, HEAD_DIM), lambda bi, h, qi: (0, 0)),
            pl.BlockSpec((LANES, 1), lambda bi, h, qi: (0, 0)),
        ],
        out_specs=pl.BlockSpec((None, DIFF_TQ, LANES), lambda bi, h, qi: (bi, qi, h)),
        out_shape=jax.ShapeDtypeStruct((b, s, DIFF_WIDTH), BF16),
        scratch_shapes=[pltpu.VMEM((DIFF_TK, 2 * DIFF_TQ), F32)] * 2,
        compiler_params=pltpu.CompilerParams(
            dimension_semantics=("arbitrary", "arbitrary", "arbitrary"),
            vmem_limit_bytes=VMEM_LIMIT),
        name="diff_attn",
    )(proj, proj, proj, lam_p, subln)


def _sb_attn_kernel(q_ref, k_ref, v_ref, o_ref, buf0, buf1):
    tq, tk, cc = SB_TQ, SB_TK, SB_CC
    qi = pl.program_id(2)
    q = q_ref[...]
    lane = lax.broadcasted_iota(jnp.int32, q.shape, 1)
    zero = jnp.zeros_like(q)
    per_q = tq // tk
    qs = jnp.concatenate([jnp.where(lane < HEAD_DIM, q, zero),
                          jnp.where(lane >= HEAD_DIM, q, zero)], axis=0)
    ss = lax.broadcasted_iota(jnp.int32, (tk, 2 * tk), 0)
    jj = lax.broadcasted_iota(jnp.int32, (tk, 2 * tk), 1) % tk
    neg_suffix = jnp.where(jj >= ss, -1.0, 0.0).astype(BF16)
    sign_bit = jnp.uint32(0x80000000)
    chunks = [slice(c * cc, (c + 1) * cc) for c in range(2 * tq // cc)]
    key = lax.broadcasted_iota(jnp.int32, (tk, cc), 0)
    qry = lax.broadcasted_iota(jnp.int32, (tk, cc), 1)

    def scores(cols, k, diag):
        q0 = cols.start % tq
        if diag is not None and q0 + cc <= diag + 1:
            return jnp.full((tk, cc), NEG_BIG, F32), jnp.zeros((1, cc), F32)
        if diag is not None and q0 >= diag + tk:
            diag = None
        z = _nt_dot(k, qs[cols])
        neg_abs = lax.bitcast_convert_type(
            lax.bitcast_convert_type(z, jnp.uint32) | sign_bit, F32)
        sp = jnp.maximum(z, 0.0) + jnp.log2(1.0 + jnp.exp2(neg_abs))
        if diag is not None:
            mask = (key + diag) < (qry + q0)
            sp = jnp.where(mask, sp, 0.0)
        hi = sp.astype(BF16)
        lo = (sp - hi.astype(F32)).astype(BF16)
        arg = z + jnp.dot(neg_suffix, jnp.concatenate([hi, lo], axis=0),
                          preferred_element_type=F32)
        if diag is not None:
            arg = jnp.where(mask, arg, NEG_BIG)
        return arg, jnp.sum(sp, axis=0, keepdims=True)

    def accumulate(arg, rem, acc, v):
        a = jnp.exp2(arg + rem)
        return acc + _tn_dot(v, a.astype(BF16))

    def step(kb_next, diag, buf_in, buf_out, carry):
        rem, acc, key_sum = carry
        k = k_ref[pl.ds(pl.multiple_of(kb_next * tk, tk), tk), :]
        v = v_ref[pl.ds(pl.multiple_of((kb_next + 1) * tk, tk), tk), :]
        rem_n, acc_n, key_sum_n = [], [], []
        for c, cols in enumerate(chunks):
            arg_next, ks = scores(cols, k, diag)
            buf_out[:, cols] = arg_next
            acc_n.append(accumulate(buf_in[:, cols], rem[c], acc[c], v))
            rem_n.append(rem[c] - key_sum[c])
            key_sum_n.append(ks)
        return rem_n, acc_n, key_sum_n

    assert per_q == 2
    d1, d0 = qi * per_q + 1, qi * per_q
    k1 = k_ref[pl.ds(pl.multiple_of(d1 * tk, tk), tk), :]
    key_sum = []
    for cols in chunks:
        arg, ks = scores(cols, k1, tk)
        buf1[:, cols] = arg
        key_sum.append(ks)
    carry = ([jnp.zeros((1, cc), F32)] * len(chunks),
             [jnp.zeros((LANES, cc), F32)] * len(chunks), key_sum)
    carry = step(d0, 0, buf1, buf0, carry)

    def live(rem):
        top = functools.reduce(jnp.maximum, rem)
        return (jnp.max(top) > SB_DEAD_LOG2).astype(jnp.int32)

    def pair(state):
        j, _, carry = state
        kb = d0 - 1 - 2 * j
        carry = step(kb, None, buf0, buf1, carry)
        carry = step(kb - 1, None, buf1, buf0, carry)
        return j + 1, live(carry[0]), carry

    n_pairs, _, (rem, acc, _) = lax.while_loop(
        lambda st: (st[0] < qi) & (st[1] > 0), pair, (jnp.int32(0), live(carry[0]), carry))
    pending = d0 - 2 * n_pairs
    v_last = v_ref[pl.ds(pl.multiple_of(pending * tk, tk), tk), :]
    acc = jnp.concatenate([accumulate(buf0[:, cols], rem[c], acc[c], v_last)
                           for c, cols in enumerate(chunks)], axis=1)
    feat = lax.broadcasted_iota(jnp.int32, (LANES, tq), 0)
    o_ref[...] = jnp.where(feat < HEAD_DIM, acc[:, :tq], acc[:, tq:]).T.astype(BF16)


def _sb_attn(proj):
    b, s, _ = proj.shape
    nq = s // SB_TQ
    qcol0 = 3 * DIFF_WIDTH // LANES
    kcol0 = qcol0 + SB_PAIRS
    vcol0 = kcol0 + SB_PAIRS
    return pl.pallas_call(
        _sb_attn_kernel,
        grid=(b, SB_PAIRS, nq),
        in_specs=[
            pl.BlockSpec((None, SB_TQ, LANES), lambda bi, p, qi: (bi, qi, qcol0 + p)),
            pl.BlockSpec((None, s, LANES), lambda bi, p, qi: (bi, 0, kcol0 + p)),
            pl.BlockSpec((None, s, LANES), lambda bi, p, qi: (bi, 0, vcol0 + p)),
        ],
        out_specs=pl.BlockSpec((None, SB_TQ, LANES), lambda bi, p, qi: (bi, qi, p)),
        out_shape=jax.ShapeDtypeStruct((b, s, SB_WIDTH), BF16),
        scratch_shapes=[pltpu.VMEM((SB_TK, 2 * SB_TQ), F32)] * 2,
        compiler_params=pltpu.CompilerParams(
            dimension_semantics=("arbitrary", "arbitrary", "arbitrary"),
            vmem_limit_bytes=VMEM_LIMIT),
        name="sb_attn",
    )(proj, proj, proj)


def _merge_kernel(x_ref, od_ref, os_ref, gd_ref, gs_ref, wd_ref, ws_ref, wo_ref, o_ref):
    pd = jnp.dot(od_ref[...], wd_ref[...], preferred_element_type=F32)
    ps = jnp.dot(os_ref[...], ws_ref[...], preferred_element_type=F32)
    merged = gd_ref[...].astype(F32) * pd + gs_ref[...].astype(F32) * ps
    o_ref[...] = x_ref[...] + jnp.dot(merged.astype(BF16), wo_ref[...],
                                      preferred_element_type=F32)


def _merge(x2d, o_diff, o_sb, proj2d, w_o_diff, w_o_sb, w_out):
    t_rows = x2d.shape[0]
    tm = MERGE_TM
    gd_blk = GATE_COL0 // D_MODEL
    const = lambda i: (0, 0)
    return pl.pallas_call(
        _merge_kernel,
        grid=(t_rows // tm,),
        in_specs=[
            pl.BlockSpec((tm, D_MODEL), lambda i: (i, 0)),
            pl.BlockSpec((tm, DIFF_WIDTH), lambda i: (i, 0)),
            pl.BlockSpec((tm, SB_WIDTH), lambda i: (i, 0)),
            pl.BlockSpec((tm, D_MODEL), lambda i: (i, gd_blk)),
            pl.BlockSpec((tm, D_MODEL), lambda i: (i, gd_blk + 1)),
            pl.BlockSpec((DIFF_WIDTH, D_MODEL), const),
            pl.BlockSpec((SB_WIDTH, D_MODEL), const),
            pl.BlockSpec((D_MODEL, D_MODEL), const),
        ],
        out_specs=pl.BlockSpec((tm, D_MODEL), lambda i: (i, 0)),
        out_shape=jax.ShapeDtypeStruct((t_rows, D_MODEL), F32),
        compiler_params=pltpu.CompilerParams(
            dimension_semantics=("arbitrary",), vmem_limit_bytes=VMEM_LIMIT),
        name="merge",
    )(x2d, o_diff, o_sb, proj2d, proj2d, w_o_diff, w_o_sb, w_out)


def _ffn_kernel(x_ref, g_ref, wg_ref, wu_ref, wd_ref, gf_ref, o_ref, h_ref, acc_ref,
                *, final_norm):
    c = pl.program_id(1)

    @pl.when(c == 0)
    def _():
        x = x_ref[...]
        ms = jnp.mean(x * x, axis=-1, keepdims=True)
        h_ref[...] = (x * lax.rsqrt(ms + NORM_EPS) * g_ref[...]).astype(BF16)
        acc_ref[...] = x

    h = h_ref[...]
    gate = jnp.dot(h, wg_ref[...], preferred_element_type=F32)
    up = jnp.dot(h, wu_ref[...], preferred_element_type=F32)
    act = (gate * jax.nn.sigmoid(gate) * up).astype(BF16)
    acc_ref[...] += jnp.dot(act, wd_ref[...], preferred_element_type=F32)

    @pl.when(c == pl.num_programs(1) - 1)
    def _():
        y = acc_ref[...]
        if final_norm:
            ms = jnp.mean(y * y, axis=-1, keepdims=True)
            y = y * lax.rsqrt(ms + NORM_EPS) * gf_ref[...]
        o_ref[...] = y


def _ffn(x2d, g, w_in, w_out, g_final, final_norm):
    t_rows = x2d.shape[0]
    tm, th = FFN_TM, FFN_TH
    n_c = FFN_HIDDEN // th
    return pl.pallas_call(
        functools.partial(_ffn_kernel, final_norm=final_norm),
        grid=(t_rows // tm, n_c),
        in_specs=[
            pl.BlockSpec((tm, D_MODEL), lambda i, c: (i, 0)),
            pl.BlockSpec((1, D_MODEL), lambda i, c: (0, 0)),
            pl.BlockSpec((D_MODEL, th), lambda i, c: (0, c)),
            pl.BlockSpec((D_MODEL, th), lambda i, c: (0, n_c + c)),
            pl.BlockSpec((th, D_MODEL), lambda i, c: (c, 0)),
            pl.BlockSpec((1, D_MODEL), lambda i, c: (0, 0)),
        ],
        out_specs=pl.BlockSpec((tm, D_MODEL), lambda i, c: (i, 0)),
        out_shape=jax.ShapeDtypeStruct((t_rows, D_MODEL), F32),
        scratch_shapes=[pltpu.VMEM((tm, D_MODEL), BF16), pltpu.VMEM((tm, D_MODEL), F32)],
        compiler_params=pltpu.CompilerParams(
            dimension_semantics=("arbitrary", "arbitrary"), vmem_limit_bytes=VMEM_LIMIT),
        name="ffn",
    )(x2d, g, w_in, w_in, w_out, g_final)


def _rope_tables(seq):
    half = HEAD_DIM // 2
    pos = jnp.arange(seq, dtype=F32)
    inv = ROPE_THETA ** (-jnp.arange(0, HEAD_DIM, 2, dtype=F32) / HEAD_DIM)
    ang = pos[:, None] * inv[None, :]
    cos, sin = jnp.cos(ang), jnp.sin(ang)
    cos_t = jnp.tile(cos, (1, LANES // half))
    sin_t = jnp.tile(jnp.concatenate([-sin, sin], axis=1), (1, LANES // HEAD_DIM))
    return cos_t, sin_t


def kernel(x, norm_attn, w_in, b_gate, diff_lambda, diff_subln, w_o_diff, w_o_sb, w_out,
           norm_ffn, w_ffn_in, w_ffn_out, norm_final):
    b, s, d = x.shape
    cos_t, sin_t = _rope_tables(s)
    x2d = x.reshape(b * s, d)
    g_final = norm_final.reshape(1, d)
    for layer in range(DEPTH):
        lam_init = 0.8 - 0.6 * math.exp(-0.3 * layer)
        bg_full = jnp.concatenate(
            [jnp.zeros((GATE_COL0,), F32), b_gate[layer]]).reshape(1, IN_COLS)
        proj2d = _in_proj(x2d, norm_attn[layer].reshape(1, d), w_in[layer].astype(BF16),
                          bg_full, cos_t, sin_t, s)
        proj = proj2d.reshape(b, s, IN_COLS)
        o_diff = _diff_attn(proj, diff_lambda[layer], diff_subln[layer].reshape(LANES, 1),
                            lam_init)
        o_sb = _sb_attn(proj)
        x2d = _merge(x2d, o_diff.reshape(b * s, DIFF_WIDTH), o_sb.reshape(b * s, SB_WIDTH),
                     proj2d, w_o_diff[layer].astype(BF16), w_o_sb[layer].astype(BF16),
                     w_out[layer].astype(BF16))
        x2d = _ffn(x2d, norm_ffn[layer].reshape(1, d), w_ffn_in[layer].astype(BF16),
                   w_ffn_out[layer].astype(BF16), g_final, layer == DEPTH - 1)
    return x2d.reshape(b, s, d)
```

```python
import functools
import math

import jax
import jax.numpy as jnp
from jax import lax
from jax.experimental import pallas as pl
from jax.experimental.pallas import tpu as pltpu

D_MODEL = 1024
DEPTH = 2
HEAD_DIM = 64
LANES = 128
DIFF_HEADS = 4
DIFF_WIDTH = DIFF_HEADS * 2 * HEAD_DIM
SB_HEADS = 8
SB_WIDTH = SB_HEADS * HEAD_DIM
SB_PAIRS = SB_WIDTH // LANES
IN_COLS = 3 * DIFF_WIDTH + 3 * SB_WIDTH + 2 * D_MODEL
GATE_COL0 = 3 * DIFF_WIDTH + 3 * SB_WIDTH
FFN_HIDDEN = 2816
ROPE_THETA = 10000.0
NORM_EPS = 1e-6
NEG_BIG = -1e30
LOG2E = math.log2(math.e)
SB_DEAD_LOG2 = -160.0

VMEM_LIMIT = 48 * 1024 * 1024

IN_TM, IN_CW = 512, 512
DIFF_TQ, DIFF_TK, DIFF_CC = 1024, 512, 256
SB_TQ, SB_TK, SB_CC = 512, 256, 256
MERGE_TM = 512
FFN_TM, FFN_TH = 1024, 1408

BF16 = jnp.bfloat16
F32 = jnp.float32


def _nt_dot(a, b):
    return lax.dot_general(a, b, (((1,), (1,)), ((), ())), preferred_element_type=F32)


def _in_proj_kernel(x_ref, g_ref, w_ref, bg_ref, cos_ref, sin_ref, o_ref):
    x = x_ref[...]
    ms = jnp.mean(x * x, axis=-1, keepdims=True)
    h = (x * lax.rsqrt(ms + NORM_EPS) * g_ref[...]).astype(BF16)

    scale = HEAD_DIM ** -0.5 * LOG2E
    cw = IN_CW
    cos = cos_ref[...]
    sin = sin_ref[...]
    lane = lax.broadcasted_iota(jnp.int32, cos.shape, 1)
    first_half = (lane % HEAD_DIM) < (HEAD_DIM // 2)

    for c0 in range(0, IN_COLS, cw):
        acc = jnp.dot(h, w_ref[:, c0:c0 + cw], preferred_element_type=F32)
        is_q = c0 < DIFF_WIDTH or 3 * DIFF_WIDTH <= c0 < 3 * DIFF_WIDTH + SB_WIDTH
        if c0 < 2 * DIFF_WIDTH:
            for b0 in range(0, cw, LANES):
                t = acc[:, b0:b0 + LANES]
                partner = jnp.where(first_half,
                                    pltpu.roll(t, LANES - HEAD_DIM // 2, 1),
                                    pltpu.roll(t, HEAD_DIM // 2, 1))
                r = t * cos + partner * sin
                if is_q:
                    r = r * scale
                o_ref[:, c0 + b0:c0 + b0 + LANES] = r.astype(BF16)
        elif c0 < GATE_COL0:
            if is_q:
                acc = acc * scale
            o_ref[:, c0:c0 + cw] = acc.astype(BF16)
        else:
            g0 = c0 - GATE_COL0
            o_ref[:, c0:c0 + cw] = jax.nn.sigmoid(acc + bg_ref[:, g0:g0 + cw]).astype(BF16)


def _in_proj(x2d, g, w, bg, cos_t, sin_t, seq):
    t_rows = x2d.shape[0]
    pos_tiles = seq // IN_TM
    return pl.pallas_call(
        _in_proj_kernel,
        grid=(t_rows // IN_TM,),
        in_specs=[
            pl.BlockSpec((IN_TM, D_MODEL), lambda i: (i, 0)),
            pl.BlockSpec((1, D_MODEL), lambda i: (0, 0)),
            pl.BlockSpec((D_MODEL, IN_COLS), lambda i: (0, 0)),
            pl.BlockSpec((1, 2 * D_MODEL), lambda i: (0, 0)),
            pl.BlockSpec((IN_TM, LANES), lambda i: (i % pos_tiles, 0)),
            pl.BlockSpec((IN_TM, LANES), lambda i: (i % pos_tiles, 0)),
        ],
        out_specs=pl.BlockSpec((IN_TM, IN_COLS), lambda i: (i, 0)),
        out_shape=jax.ShapeDtypeStruct((t_rows, IN_COLS), BF16),
        compiler_params=pltpu.CompilerParams(
            dimension_semantics=("arbitrary",), vmem_limit_bytes=VMEM_LIMIT),
        name="in_proj",
    )(x2d, g, w, bg, cos_t, sin_t)


def _tn_dot(a, b):
    return lax.dot_general(a, b, (((0,), (0,)), ((), ())), preferred_element_type=F32)


def _diff_attn_kernel(q_ref, k_ref, v_ref, lam_ref, subln_ref, o_ref, buf0, buf1, *,
                      lam_init):
    tq, tk, cc = DIFF_TQ, DIFF_TK, DIFF_CC
    qi = pl.program_id(2)
    q = q_ref[...]
    lane = lax.broadcasted_iota(jnp.int32, q.shape, 1)
    zero = jnp.zeros_like(q)
    qs = jnp.concatenate([jnp.where(lane < HEAD_DIM, q, zero),
                          jnp.where(lane >= HEAD_DIM, q, zero)], axis=0)
    chunks = [slice(c * cc, (c + 1) * cc) for c in range(2 * tq // cc)]
    key = lax.broadcasted_iota(jnp.int32, (tk, cc), 0)
    qry = lax.broadcasted_iota(jnp.int32, (tk, cc), 1)

    def scores(cols, k, diag):
        q0 = cols.start % tq
        if diag is not None and q0 + cc <= diag:
            return jnp.full((tk, cc), NEG_BIG, F32), jnp.full((1, cc), NEG_BIG, F32)
        s = _nt_dot(k, qs[cols])
        if diag is not None and q0 < diag + tk - 1:
            s = jnp.where((key + diag) <= (qry + q0), s, NEG_BIG)
        return s, jnp.max(s, axis=0, keepdims=True)

    def accumulate(s, tile_max, m, l, acc, v):
        m_new = jnp.maximum(m, tile_max)
        alpha = jnp.exp2(m - m_new)
        p = jnp.exp2(s - m_new)
        l = alpha * l + jnp.sum(p, axis=0, keepdims=True)
        acc = alpha * acc + _tn_dot(v, p.astype(BF16))
        return m_new, l, acc

    def step(kb_next, kb_pending, diag, buf_in, buf_out, carry):
        m, l, acc, tile_max = carry
        k = k_ref[pl.ds(pl.multiple_of(kb_next * tk, tk), tk), :]
        v = v_ref[pl.ds(pl.multiple_of(kb_pending * tk, tk), tk), :]
        m_n, l_n, acc_n, tile_max_n = [], [], [], []
        for c, cols in enumerate(chunks):
            s_next, mx = scores(cols, k, diag)
            buf_out[:, cols] = s_next
            mc, lc, ac = accumulate(buf_in[:, cols], tile_max[c], m[c], l[c], acc[c], v)
            m_n.append(mc)
            l_n.append(lc)
            acc_n.append(ac)
            tile_max_n.append(mx)
        return m_n, l_n, acc_n, tile_max_n

    per_q = tq // tk
    assert per_q == 2
    d0, d1 = qi * per_q, qi * per_q + 1
    kd0 = k_ref[pl.ds(pl.multiple_of(d0 * tk, tk), tk), :]
    tile_max = []
    for cols in chunks:
        s, mx = scores(cols, kd0, 0)
        buf0[:, cols] = s
        tile_max.append(mx)
    n_c = len(chunks)
    carry = ([jnp.full((1, cc), NEG_BIG, F32)] * n_c, [jnp.zeros((1, cc), F32)] * n_c,
             [jnp.zeros((LANES, cc), F32)] * n_c, tile_max)
    carry = step(d1, d0, tk, buf0, buf1, carry)

    def pair(j, carry):
        kb = d0 - 1 - 2 * j
        carry = step(kb, jnp.where(j == 0, d1, kb + 1), None, buf1, buf0, carry)
        return step(kb - 1, kb, None, buf0, buf1, carry)

    m, l, acc, tile_max = lax.fori_loop(0, qi, pair, carry)
    kb_last = jnp.where(qi == 0, d1, 0)
    v_last = v_ref[pl.ds(pl.multiple_of(kb_last * tk, tk), tk), :]
    outs = [accumulate(buf1[:, cols], tile_max[c], m[c], l[c], acc[c], v_last)
            for c, cols in enumerate(chunks)]
    o = jnp.concatenate([oc[2] / oc[1] for oc in outs], axis=1)

    lp = lam_ref[...]
    e1 = jnp.exp(jnp.sum(lp[0:1, :] * lp[1:2, :], axis=1, keepdims=True))
    e2 = jnp.exp(jnp.sum(lp[2:3, :] * lp[3:4, :], axis=1, keepdims=True))
    lam = e1 - e2 + lam_init
    o = o[:, :tq] - lam * o[:, tq:]
    ms = jnp.mean(o * o, axis=0, keepdims=True)
    o = o * lax.rsqrt(ms + NORM_EPS) * subln_ref[...] * (1.0 - lam_init)
    o_ref[...] = o.T.astype(BF16)


def _diff_attn(proj, lam_p, subln, lam_init):
    b, s, _ = proj.shape
    nq = s // DIFF_TQ
    kcol0 = DIFF_WIDTH // LANES
    vcol0 = 2 * DIFF_WIDTH // LANES
    return pl.pallas_call(
        functools.partial(_diff_attn_kernel, lam_init=lam_init),
        grid=(b, DIFF_HEADS, nq),
        in_specs=[
            pl.BlockSpec((None, DIFF_TQ, LANES), lambda bi, h, qi: (bi, qi, h)),
            pl.BlockSpec((None, s, LANES), lambda bi, h, qi: (bi, 0, kcol0 + h)),
            pl.BlockSpec((None, s, LANES), lambda bi, h, qi: (bi, 0, vcol0 + h)),
            pl.BlockSpec((4, HEAD_DIM), lambda bi, h, qi: (0, 0)),
            pl.BlockSpec((LANES, 1), lambda bi, h, qi: (0, 0)),
        ],
        out_specs=pl.BlockSpec((None, DIFF_TQ, LANES), lambda bi, h, qi: (bi, qi, h)),
        out_shape=jax.ShapeDtypeStruct((b, s, DIFF_WIDTH), BF16),
        scratch_shapes=[pltpu.VMEM((DIFF_TK, 2 * DIFF_TQ), F32)] * 2,
        compiler_params=pltpu.CompilerParams(
            dimension_semantics=("arbitrary", "arbitrary", "arbitrary"),
            vmem_limit_bytes=VMEM_LIMIT),
        name="diff_attn",
    )(proj, proj, proj, lam_p, subln)


def _sb_attn_kernel(q_ref, k_ref, v_ref, o_ref, buf0, buf1):
    tq, tk, cc = SB_TQ, SB_TK, SB_CC
    qi = pl.program_id(2)
    q = q_ref[...]
    lane = lax.broadcasted_iota(jnp.int32, q.shape, 1)
    zero = jnp.zeros_like(q)
    per_q = tq // tk
    qs = jnp.concatenate([jnp.where(lane < HEAD_DIM, q, zero),
                          jnp.where(lane >= HEAD_DIM, q, zero)], axis=0)
    ss = lax.broadcasted_iota(jnp.int32, (tk, 2 * tk), 0)
    jj = lax.broadcasted_iota(jnp.int32, (tk, 2 * tk), 1) % tk
    neg_suffix = jnp.where(jj >= ss, -1.0, 0.0).astype(BF16)
    sign_bit = jnp.uint32(0x80000000)
    chunks = [slice(c * cc, (c + 1) * cc) for c in range(2 * tq // cc)]
    key = lax.broadcasted_iota(jnp.int32, (tk, cc), 0)
    qry = lax.broadcasted_iota(jnp.int32, (tk, cc), 1)

    def scores(cols, k, diag):
        q0 = cols.start % tq
        if diag is not None and q0 + cc <= diag + 1:
            return jnp.full((tk, cc), NEG_BIG, F32), jnp.zeros((1, cc), F32)
        if diag is not None and q0 >= diag + tk:
            diag = None
        z = _nt_dot(k, qs[cols])
        neg_abs = lax.bitcast_convert_type(
            lax.bitcast_convert_type(z, jnp.uint32) | sign_bit, F32)
        sp = jnp.maximum(z, 0.0) + jnp.log2(1.0 + jnp.exp2(neg_abs))
        if diag is not None:
            mask = (key + diag) < (qry + q0)
            sp = jnp.where(mask, sp, 0.0)
        hi = sp.astype(BF16)
        lo = (sp - hi.astype(F32)).astype(BF16)
        arg = z + jnp.dot(neg_suffix, jnp.concatenate([hi, lo], axis=0),
                          preferred_element_type=F32)
        if diag is not None:
            arg = jnp.where(mask, arg, NEG_BIG)
        return arg, jnp.sum(sp, axis=0, keepdims=True)

    def accumulate(arg, rem, acc, v):
        a = jnp.exp2(arg + rem)
        return acc + _tn_dot(v, a.astype(BF16))

    def step(kb_next, diag, buf_in, buf_out, carry):
        rem, acc, key_sum = carry
        k = k_ref[pl.ds(pl.multiple_of(kb_next * tk, tk), tk), :]
        v = v_ref[pl.ds(pl.multiple_of((kb_next + 1) * tk, tk), tk), :]
        rem_n, acc_n, key_sum_n = [], [], []
        for c, cols in enumerate(chunks):
            arg_next, ks = scores(cols, k, diag)
            buf_out[:, cols] = arg_next
            acc_n.append(accumulate(buf_in[:, cols], rem[c], acc[c], v))
            rem_n.append(rem[c] - key_sum[c])
            key_sum_n.append(ks)
        return rem_n, acc_n, key_sum_n

    assert per_q == 2
    d1, d0 = qi * per_q + 1, qi * per_q
    k1 = k_ref[pl.ds(pl.multiple_of(d1 * tk, tk), tk), :]
    key_sum = []
    for cols in chunks:
        arg, ks = scores(cols, k1, tk)
        buf1[:, cols] = arg
        key_sum.append(ks)
    carry = ([jnp.zeros((1, cc), F32)] * len(chunks),
             [jnp.zeros((LANES, cc), F32)] * len(chunks), key_sum)
    carry = step(d0, 0, buf1, buf0, carry)

    def live(rem):
        top = functools.reduce(jnp.maximum, rem)
        return (jnp.max(top) > SB_DEAD_LOG2).astype(jnp.int32)

    def pair(state):
        j, _, carry = state
        kb = d0 - 1 - 2 * j
        carry = step(kb, None, buf0, buf1, carry)
        carry = step(kb - 1, None, buf1, buf0, carry)
        return j + 1, live(carry[0]), carry

    n_pairs, _, (rem, acc, _) = lax.while_loop(
        lambda st: (st[0] < qi) & (st[1] > 0), pair, (jnp.int32(0), live(carry[0]), carry))
    pending = d0 - 2 * n_pairs
    v_last = v_ref[pl.ds(pl.multiple_of(pending * tk, tk), tk), :]
    acc = jnp.concatenate([accumulate(buf0[:, cols], rem[c], acc[c], v_last)
                           for c, cols in enumerate(chunks)], axis=1)
    feat = lax.broadcasted_iota(jnp.int32, (LANES, tq), 0)
    o_ref[...] = jnp.where(feat < HEAD_DIM, acc[:, :tq], acc[:, tq:]).T.astype(BF16)


def _sb_attn(proj):
    b, s, _ = proj.shape
    nq = s // SB_TQ
    qcol0 = 3 * DIFF_WIDTH // LANES
    kcol0 = qcol0 + SB_PAIRS
    vcol0 = kcol0 + SB_PAIRS
    return pl.pallas_call(
        _sb_attn_kernel,
        grid=(b, SB_PAIRS, nq),
        in_specs=[
            pl.BlockSpec((None, SB_TQ, LANES), lambda bi, p, qi: (bi, qi, qcol0 + p)),
            pl.BlockSpec((None, s, LANES), lambda bi, p, qi: (bi, 0, kcol0 + p)),
            pl.BlockSpec((None, s, LANES), lambda bi, p, qi: (bi, 0, vcol0 + p)),
        ],
        out_specs=pl.BlockSpec((None, SB_TQ, LANES), lambda bi, p, qi: (bi, qi, p)),
        out_shape=jax.ShapeDtypeStruct((b, s, SB_WIDTH), BF16),
        scratch_shapes=[pltpu.VMEM((SB_TK, 2 * SB_TQ), F32)] * 2,
        compiler_params=pltpu.CompilerParams(
            dimension_semantics=("arbitrary", "arbitrary", "arbitrary"),
            vmem_limit_bytes=VMEM_LIMIT),
        name="sb_attn",
    )(proj, proj, proj)


def _merge_kernel(x_ref, od_ref, os_ref, gd_ref, gs_ref, wd_ref, ws_ref, wo_ref, o_ref):
    pd = jnp.dot(od_ref[...], wd_ref[...], preferred_element_type=F32)
    ps = jnp.dot(os_ref[...], ws_ref[...], preferred_element_type=F32)
    merged = gd_ref[...].astype(F32) * pd + gs_ref[...].astype(F32) * ps
    o_ref[...] = x_ref[...] + jnp.dot(merged.astype(BF16), wo_ref[...],
                                      preferred_element_type=F32)


def _merge(x2d, o_diff, o_sb, proj2d, w_o_diff, w_o_sb, w_out):
    t_rows = x2d.shape[0]
    tm = MERGE_TM
    gd_blk = GATE_COL0 // D_MODEL
    const = lambda i: (0, 0)
    return pl.pallas_call(
        _merge_kernel,
        grid=(t_rows // tm,),
        in_specs=[
            pl.BlockSpec((tm, D_MODEL), lambda i: (i, 0)),
            pl.BlockSpec((tm, DIFF_WIDTH), lambda i: (i, 0)),
            pl.BlockSpec((tm, SB_WIDTH), lambda i: (i, 0)),
            pl.BlockSpec((tm, D_MODEL), lambda i: (i, gd_blk)),
            pl.BlockSpec((tm, D_MODEL), lambda i: (i, gd_blk + 1)),
            pl.BlockSpec((DIFF_WIDTH, D_MODEL), const),
            pl.BlockSpec((SB_WIDTH, D_MODEL), const),
            pl.BlockSpec((D_MODEL, D_MODEL), const),
        ],
        out_specs=pl.BlockSpec((tm, D_MODEL), lambda i: (i, 0)),
        out_shape=jax.ShapeDtypeStruct((t_rows, D_MODEL), F32),
        compiler_params=pltpu.CompilerParams(
            dimension_semantics=("arbitrary",), vmem_limit_bytes=VMEM_LIMIT),
        name="merge",
    )(x2d, o_diff, o_sb, proj2d, proj2d, w_o_diff, w_o_sb, w_out)


def _ffn_kernel(x_ref, g_ref, wg_ref, wu_ref, wd_ref, gf_ref, o_ref, h_ref, acc_ref,
                *, final_norm):
    c = pl.program_id(1)

    @pl.when(c == 0)
    def _():
        x = x_ref[...]
        ms = jnp.mean(x * x, axis=-1, keepdims=True)
        h_ref[...] = (x * lax.rsqrt(ms + NORM_EPS) * g_ref[...]).astype(BF16)
        acc_ref[...] = x

    h = h_ref[...]
    gate = jnp.dot(h, wg_ref[...], preferred_element_type=F32)
    up = jnp.dot(h, wu_ref[...], preferred_element_type=F32)
    act = (gate * jax.nn.sigmoid(gate) * up).astype(BF16)
    acc_ref[...] += jnp.dot(act, wd_ref[...], preferred_element_type=F32)

    @pl.when(c == pl.num_programs(1) - 1)
    def _():
        y = acc_ref[...]
        if final_norm:
            ms = jnp.mean(y * y, axis=-1, keepdims=True)
            y = y * lax.rsqrt(ms + NORM_EPS) * gf_ref[...]
        o_ref[...] = y


def _ffn(x2d, g, w_in, w_out, g_final, final_norm):
    t_rows = x2d.shape[0]
    tm, th = FFN_TM, FFN_TH
    n_c = FFN_HIDDEN // th
    return pl.pallas_call(
        functools.partial(_ffn_kernel, final_norm=final_norm),
        grid=(t_rows // tm, n_c),
        in_specs=[
            pl.BlockSpec((tm, D_MODEL), lambda i, c: (i, 0)),
            pl.BlockSpec((1, D_MODEL), lambda i, c: (0, 0)),
            pl.BlockSpec((D_MODEL, th), lambda i, c: (0, c)),
            pl.BlockSpec((D_MODEL, th), lambda i, c: (0, n_c + c)),
            pl.BlockSpec((th, D_MODEL), lambda i, c: (c, 0)),
            pl.BlockSpec((1, D_MODEL), lambda i, c: (0, 0)),
        ],
        out_specs=pl.BlockSpec((tm, D_MODEL), lambda i, c: (i, 0)),
        out_shape=jax.ShapeDtypeStruct((t_rows, D_MODEL), F32),
        scratch_shapes=[pltpu.VMEM((tm, D_MODEL), BF16), pltpu.VMEM((tm, D_MODEL), F32)],
        compiler_params=pltpu.CompilerParams(
            dimension_semantics=("arbitrary", "arbitrary"), vmem_limit_bytes=VMEM_LIMIT),
        name="ffn",
    )(x2d, g, w_in, w_in, w_out, g_final)


def _rope_tables(seq):
    half = HEAD_DIM // 2
    pos = jnp.arange(seq, dtype=F32)
    inv = ROPE_THETA ** (-jnp.arange(0, HEAD_DIM, 2, dtype=F32) / HEAD_DIM)
    ang = pos[:, None] * inv[None, :]
    cos, sin = jnp.cos(ang), jnp.sin(ang)
    cos_t = jnp.tile(cos, (1, LANES // half))
    sin_t = jnp.tile(jnp.concatenate([-sin, sin], axis=1), (1, LANES // HEAD_DIM))
    return cos_t, sin_t


def kernel(x, norm_attn, w_in, b_gate, diff_lambda, diff_subln, w_o_diff, w_o_sb, w_out,
           norm_ffn, w_ffn_in, w_ffn_out, norm_final):
    b, s, d = x.shape
    cos_t, sin_t = _rope_tables(s)
    x2d = x.reshape(b * s, d)
    g_final = norm_final.reshape(1, d)
    for layer in range(DEPTH):
        lam_init = 0.8 - 0.6 * math.exp(-0.3 * layer)
        proj2d = _in_proj(x2d, norm_attn[layer].reshape(1, d), w_in[layer].astype(BF16),
                          b_gate[layer].reshape(1, 2 * d), cos_t, sin_t, s)
        proj = proj2d.reshape(b, s, IN_COLS)
        o_diff = _diff_attn(proj, diff_lambda[layer], diff_subln[layer].reshape(LANES, 1),
                            lam_init)
        o_sb = _sb_attn(proj)
        x2d = _merge(x2d, o_diff.reshape(b * s, DIFF_WIDTH), o_sb.reshape(b * s, SB_WIDTH),
                     proj2d, w_o_diff[layer].astype(BF16), w_o_sb[layer].astype(BF16),
                     w_out[layer].astype(BF16))
        x2d = _ffn(x2d, norm_ffn[layer].reshape(1, d), w_ffn_in[layer].astype(BF16),
                   w_ffn_out[layer].astype(BF16), g_final, layer == DEPTH - 1)
    return x2d.reshape(b, s, d)
```

```python
import functools
import math

import jax
import jax.numpy as jnp
from jax import lax
from jax.experimental import pallas as pl
from jax.experimental.pallas import tpu as pltpu

D_MODEL = 1024
DEPTH = 2
HEAD_DIM = 64
LANES = 128
DIFF_HEADS = 4
DIFF_WIDTH = DIFF_HEADS * 2 * HEAD_DIM
SB_HEADS = 8
SB_WIDTH = SB_HEADS * HEAD_DIM
SB_PAIRS = SB_WIDTH // LANES
IN_COLS = 3 * DIFF_WIDTH + 3 * SB_WIDTH + 2 * D_MODEL
GATE_COL0 = 3 * DIFF_WIDTH + 3 * SB_WIDTH
FFN_HIDDEN = 2816
ROPE_THETA = 10000.0
NORM_EPS = 1e-6
NEG_BIG = -1e30
LOG2E = math.log2(math.e)
SB_DEAD_LOG2 = -160.0

VMEM_LIMIT = 48 * 1024 * 1024

IN_TM, IN_CW = 512, 512
DIFF_TQ, DIFF_TK, DIFF_CC = 1024, 512, 256
SB_TQ, SB_TK, SB_CC = 512, 256, 256
MERGE_TM = 512
FFN_TM, FFN_CH = 512, 512

BF16 = jnp.bfloat16
F32 = jnp.float32


def _nt_dot(a, b):
    return lax.dot_general(a, b, (((1,), (1,)), ((), ())), preferred_element_type=F32)


def _in_proj_kernel(x_ref, g_ref, w_ref, bg_ref, cos_ref, sin_ref, o_ref):
    x = x_ref[...]
    ms = jnp.mean(x * x, axis=-1, keepdims=True)
    h = (x * lax.rsqrt(ms + NORM_EPS) * g_ref[...]).astype(BF16)

    scale = HEAD_DIM ** -0.5 * LOG2E
    cw = IN_CW
    cos = cos_ref[...]
    sin = sin_ref[...]
    lane = lax.broadcasted_iota(jnp.int32, cos.shape, 1)
    first_half = (lane % HEAD_DIM) < (HEAD_DIM // 2)

    for c0 in range(0, IN_COLS, cw):
        acc = jnp.dot(h, w_ref[:, c0:c0 + cw], preferred_element_type=F32)
        is_q = c0 < DIFF_WIDTH or 3 * DIFF_WIDTH <= c0 < 3 * DIFF_WIDTH + SB_WIDTH
        if c0 < 2 * DIFF_WIDTH:
            for b0 in range(0, cw, LANES):
                t = acc[:, b0:b0 + LANES]
                partner = jnp.where(first_half,
                                    pltpu.roll(t, LANES - HEAD_DIM // 2, 1),
                                    pltpu.roll(t, HEAD_DIM // 2, 1))
                r = t * cos + partner * sin
                if is_q:
                    r = r * scale
                o_ref[:, c0 + b0:c0 + b0 + LANES] = r.astype(BF16)
        elif c0 < GATE_COL0:
            if is_q:
                acc = acc * scale
            o_ref[:, c0:c0 + cw] = acc.astype(BF16)
        else:
            g0 = c0 - GATE_COL0
            o_ref[:, c0:c0 + cw] = jax.nn.sigmoid(acc + bg_ref[:, g0:g0 + cw]).astype(BF16)


def _in_proj(x2d, g, w, bg, cos_t, sin_t, seq):
    t_rows = x2d.shape[0]
    pos_tiles = seq // IN_TM
    return pl.pallas_call(
        _in_proj_kernel,
        grid=(t_rows // IN_TM,),
        in_specs=[
            pl.BlockSpec((IN_TM, D_MODEL), lambda i: (i, 0)),
            pl.BlockSpec((1, D_MODEL), lambda i: (0, 0)),
            pl.BlockSpec((D_MODEL, IN_COLS), lambda i: (0, 0)),
            pl.BlockSpec((1, 2 * D_MODEL), lambda i: (0, 0)),
            pl.BlockSpec((IN_TM, LANES), lambda i: (i % pos_tiles, 0)),
            pl.BlockSpec((IN_TM, LANES), lambda i: (i % pos_tiles, 0)),
        ],
        out_specs=pl.BlockSpec((IN_TM, IN_COLS), lambda i: (i, 0)),
        out_shape=jax.ShapeDtypeStruct((t_rows, IN_COLS), BF16),
        compiler_params=pltpu.CompilerParams(
            dimension_semantics=("arbitrary",), vmem_limit_bytes=VMEM_LIMIT),
        name="in_proj",
    )(x2d, g, w, bg, cos_t, sin_t)


def _tn_dot(a, b):
    return lax.dot_general(a, b, (((0,), (0,)), ((), ())), preferred_element_type=F32)


def _diff_attn_kernel(q_ref, k_ref, v_ref, lam_ref, subln_ref, o_ref, buf0, buf1, *,
                      lam_init):
    tq, tk, cc = DIFF_TQ, DIFF_TK, DIFF_CC
    qi = pl.program_id(2)
    q = q_ref[...]
    lane = lax.broadcasted_iota(jnp.int32, q.shape, 1)
    zero = jnp.zeros_like(q)
    qs = jnp.concatenate([jnp.where(lane < HEAD_DIM, q, zero),
                          jnp.where(lane >= HEAD_DIM, q, zero)], axis=0)
    chunks = [slice(c * cc, (c + 1) * cc) for c in range(2 * tq // cc)]
    key = lax.broadcasted_iota(jnp.int32, (tk, cc), 0)
    qry = lax.broadcasted_iota(jnp.int32, (tk, cc), 1)

    def scores(cols, k, diag):
        q0 = cols.start % tq
        if diag is not None and q0 + cc <= diag:
            return None
        s = _nt_dot(k, qs[cols])
        if diag is not None and q0 < diag + tk - 1:
            s = jnp.where((key + diag) <= (qry + q0), s, NEG_BIG)
        return s, jnp.max(s, axis=0, keepdims=True)

    def accumulate(s, tile_max, m, l, acc, v):
        m_new = jnp.maximum(m, tile_max)
        alpha = jnp.exp2(m - m_new)
        p = jnp.exp2(s - m_new)
        l = alpha * l + jnp.sum(p, axis=0, keepdims=True)
        acc = alpha * acc + _tn_dot(v, p.astype(BF16))
        return m_new, l, acc

    def step(kb_next, kb_pending, diag, buf_in, buf_out, carry):
        m, l, acc, tile_max = carry
        k = k_ref[pl.ds(pl.multiple_of(kb_next * tk, tk), tk), :]
        v = v_ref[pl.ds(pl.multiple_of(kb_pending * tk, tk), tk), :]
        m_n, l_n, acc_n, tile_max_n = [], [], [], []
        for c, cols in enumerate(chunks):
            scored = scores(cols, k, diag)
            if scored is not None:
                buf_out[:, cols] = scored[0]
            state = (m[c], l[c], acc[c])
            if tile_max[c] is not None:
                state = accumulate(buf_in[:, cols], tile_max[c], *state, v)
            m_n.append(state[0])
            l_n.append(state[1])
            acc_n.append(state[2])
            tile_max_n.append(None if scored is None else scored[1])
        return m_n, l_n, acc_n, tile_max_n

    per_q = tq // tk
    assert per_q == 2
    d0, d1 = qi * per_q, qi * per_q + 1
    kd1 = k_ref[pl.ds(pl.multiple_of(d1 * tk, tk), tk), :]
    tile_max = []
    for cols in chunks:
        scored = scores(cols, kd1, tk)
        if scored is not None:
            buf1[:, cols] = scored[0]
        tile_max.append(None if scored is None else scored[1])
    n_c = len(chunks)
    carry = ([jnp.full((1, cc), NEG_BIG, F32)] * n_c, [jnp.zeros((1, cc), F32)] * n_c,
             [jnp.zeros((LANES, cc), F32)] * n_c, tile_max)
    carry = step(d0, d1, 0, buf1, buf0, carry)

    def pair(j, carry):
        kb = d0 - 1 - 2 * j
        carry = step(kb, kb + 1, None, buf0, buf1, carry)
        return step(kb - 1, kb, None, buf1, buf0, carry)

    m, l, acc, tile_max = lax.fori_loop(0, qi, pair, carry)
    v_last = v_ref[pl.ds(0, tk), :]
    outs = [accumulate(buf0[:, cols], tile_max[c], m[c], l[c], acc[c], v_last)
            for c, cols in enumerate(chunks)]
    o = jnp.concatenate([oc[2] / oc[1] for oc in outs], axis=1)

    lp = lam_ref[...]
    e1 = jnp.exp(jnp.sum(lp[0:1, :] * lp[1:2, :], axis=1, keepdims=True))
    e2 = jnp.exp(jnp.sum(lp[2:3, :] * lp[3:4, :], axis=1, keepdims=True))
    lam = e1 - e2 + lam_init
    o = o[:, :tq] - lam * o[:, tq:]
    ms = jnp.mean(o * o, axis=0, keepdims=True)
    o = o * lax.rsqrt(ms + NORM_EPS) * subln_ref[...] * (1.0 - lam_init)
    o_ref[...] = o.T.astype(BF16)


def _diff_attn(proj, lam_p, subln, lam_init):
    b, s, _ = proj.shape
    nq = s // DIFF_TQ
    kcol0 = DIFF_WIDTH // LANES
    vcol0 = 2 * DIFF_WIDTH // LANES
    return pl.pallas_call(
        functools.partial(_diff_attn_kernel, lam_init=lam_init),
        grid=(b, DIFF_HEADS, nq),
        in_specs=[
            pl.BlockSpec((None, DIFF_TQ, LANES), lambda bi, h, qi: (bi, qi, h)),
            pl.BlockSpec((None, s, LANES), lambda bi, h, qi: (bi, 0, kcol0 + h)),
            pl.BlockSpec((None, s, LANES), lambda bi, h, qi: (bi, 0, vcol0 + h)),
            pl.BlockSpec((4, HEAD_DIM), lambda bi, h, qi: (0, 0)),
            pl.BlockSpec((LANES, 1), lambda bi, h, qi: (0, 0)),
        ],
        out_specs=pl.BlockSpec((None, DIFF_TQ, LANES), lambda bi, h, qi: (bi, qi, h)),
        out_shape=jax.ShapeDtypeStruct((b, s, DIFF_WIDTH), BF16),
        scratch_shapes=[pltpu.VMEM((DIFF_TK, 2 * DIFF_TQ), F32)] * 2,
        compiler_params=pltpu.CompilerParams(
            dimension_semantics=("arbitrary", "arbitrary", "arbitrary"),
            vmem_limit_bytes=VMEM_LIMIT),
        name="diff_attn",
    )(proj, proj, proj, lam_p, subln)


def _sb_attn_kernel(q_ref, k_ref, v_ref, o_ref, buf0, buf1):
    tq, tk, cc = SB_TQ, SB_TK, SB_CC
    qi = pl.program_id(2)
    q = q_ref[...]
    lane = lax.broadcasted_iota(jnp.int32, q.shape, 1)
    zero = jnp.zeros_like(q)
    per_q = tq // tk
    qs = jnp.concatenate([jnp.where(lane < HEAD_DIM, q, zero),
                          jnp.where(lane >= HEAD_DIM, q, zero)], axis=0)
    ss = lax.broadcasted_iota(jnp.int32, (tk, 2 * tk), 0)
    jj = lax.broadcasted_iota(jnp.int32, (tk, 2 * tk), 1) % tk
    neg_suffix = jnp.where(jj >= ss, -1.0, 0.0).astype(BF16)
    sign_bit = jnp.uint32(0x80000000)
    chunks = [slice(c * cc, (c + 1) * cc) for c in range(2 * tq // cc)]
    key = lax.broadcasted_iota(jnp.int32, (tk, cc), 0)
    qry = lax.broadcasted_iota(jnp.int32, (tk, cc), 1)

    def scores(cols, k, diag):
        q0 = cols.start % tq
        if diag is not None and q0 + cc <= diag + 1:
            return None, jnp.zeros((1, cc), F32)
        if diag is not None and q0 >= diag + tk:
            diag = None
        z = _nt_dot(k, qs[cols])
        neg_abs = lax.bitcast_convert_type(
            lax.bitcast_convert_type(z, jnp.uint32) | sign_bit, F32)
        sp = jnp.maximum(z, 0.0) + jnp.log2(1.0 + jnp.exp2(neg_abs))
        if diag is not None:
            mask = (key + diag) < (qry + q0)
            sp = jnp.where(mask, sp, 0.0)
        hi = sp.astype(BF16)
        lo = (sp - hi.astype(F32)).astype(BF16)
        arg = z + jnp.dot(neg_suffix, jnp.concatenate([hi, lo], axis=0),
                          preferred_element_type=F32)
        if diag is not None:
            arg = jnp.where(mask, arg, NEG_BIG)
        return arg, jnp.sum(sp, axis=0, keepdims=True)

    def accumulate(arg, rem, acc, v):
        a = jnp.exp2(arg + rem)
        return acc + _tn_dot(v, a.astype(BF16))

    def step(kb_next, diag, buf_in, buf_out, carry, pending_empty=()):
        rem, acc, key_sum = carry
        k = k_ref[pl.ds(pl.multiple_of(kb_next * tk, tk), tk), :]
        v = v_ref[pl.ds(pl.multiple_of((kb_next + 1) * tk, tk), tk), :]
        rem_n, acc_n, key_sum_n = [], [], []
        for c, cols in enumerate(chunks):
            arg_next, ks = scores(cols, k, diag)
            buf_out[:, cols] = arg_next
            acc_n.append(acc[c] if c in pending_empty
                         else accumulate(buf_in[:, cols], rem[c], acc[c], v))
            rem_n.append(rem[c] - key_sum[c])
            key_sum_n.append(ks)
        return rem_n, acc_n, key_sum_n

    assert per_q == 2
    d1, d0 = qi * per_q + 1, qi * per_q
    k1 = k_ref[pl.ds(pl.multiple_of(d1 * tk, tk), tk), :]
    key_sum, empty = [], []
    for c, cols in enumerate(chunks):
        arg, ks = scores(cols, k1, tk)
        if arg is None:
            empty.append(c)
        else:
            buf1[:, cols] = arg
        key_sum.append(ks)
    carry = ([jnp.zeros((1, cc), F32)] * len(chunks),
             [jnp.zeros((LANES, cc), F32)] * len(chunks), key_sum)
    carry = step(d0, 0, buf1, buf0, carry, pending_empty=empty)

    def live(rem):
        top = functools.reduce(jnp.maximum, rem)
        return (jnp.max(top) > SB_DEAD_LOG2).astype(jnp.int32)

    def pair(state):
        j, _, carry = state
        kb = d0 - 1 - 2 * j
        carry = step(kb, None, buf0, buf1, carry)
        carry = step(kb - 1, None, buf1, buf0, carry)
        return j + 1, live(carry[0]), carry

    n_pairs, _, (rem, acc, _) = lax.while_loop(
        lambda st: (st[0] < qi) & (st[1] > 0), pair, (jnp.int32(0), live(carry[0]), carry))
    pending = d0 - 2 * n_pairs
    v_last = v_ref[pl.ds(pl.multiple_of(pending * tk, tk), tk), :]
    acc = jnp.concatenate([accumulate(buf0[:, cols], rem[c], acc[c], v_last)
                           for c, cols in enumerate(chunks)], axis=1)
    feat = lax.broadcasted_iota(jnp.int32, (LANES, tq), 0)
    o_ref[...] = jnp.where(feat < HEAD_DIM, acc[:, :tq], acc[:, tq:]).T.astype(BF16)


def _sb_attn(proj):
    b, s, _ = proj.shape
    nq = s // SB_TQ
    qcol0 = 3 * DIFF_WIDTH // LANES
    kcol0 = qcol0 + SB_PAIRS
    vcol0 = kcol0 + SB_PAIRS
    return pl.pallas_call(
        _sb_attn_kernel,
        grid=(b, SB_PAIRS, nq),
        in_specs=[
            pl.BlockSpec((None, SB_TQ, LANES), lambda bi, p, qi: (bi, qi, qcol0 + p)),
            pl.BlockSpec((None, s, LANES), lambda bi, p, qi: (bi, 0, kcol0 + p)),
            pl.BlockSpec((None, s, LANES), lambda bi, p, qi: (bi, 0, vcol0 + p)),
        ],
        out_specs=pl.BlockSpec((None, SB_TQ, LANES), lambda bi, p, qi: (bi, qi, p)),
        out_shape=jax.ShapeDtypeStruct((b, s, SB_WIDTH), BF16),
        scratch_shapes=[pltpu.VMEM((SB_TK, 2 * SB_TQ), F32)] * 2,
        compiler_params=pltpu.CompilerParams(
            dimension_semantics=("arbitrary", "arbitrary", "arbitrary"),
            vmem_limit_bytes=VMEM_LIMIT),
        name="sb_attn",
    )(proj, proj, proj)


def _merge_kernel(x_ref, od_ref, os_ref, gd_ref, gs_ref, wd_ref, ws_ref, wo_ref, o_ref):
    pd = jnp.dot(od_ref[...], wd_ref[...], preferred_element_type=F32)
    ps = jnp.dot(os_ref[...], ws_ref[...], preferred_element_type=F32)
    merged = gd_ref[...].astype(F32) * pd + gs_ref[...].astype(F32) * ps
    o_ref[...] = x_ref[...] + jnp.dot(merged.astype(BF16), wo_ref[...],
                                      preferred_element_type=F32)


def _merge(x2d, o_diff, o_sb, proj2d, w_o_diff, w_o_sb, w_out):
    t_rows = x2d.shape[0]
    tm = MERGE_TM
    gd_blk = GATE_COL0 // D_MODEL
    const = lambda i: (0, 0)
    return pl.pallas_call(
        _merge_kernel,
        grid=(t_rows // tm,),
        in_specs=[
            pl.BlockSpec((tm, D_MODEL), lambda i: (i, 0)),
            pl.BlockSpec((tm, DIFF_WIDTH), lambda i: (i, 0)),
            pl.BlockSpec((tm, SB_WIDTH), lambda i: (i, 0)),
            pl.BlockSpec((tm, D_MODEL), lambda i: (i, gd_blk)),
            pl.BlockSpec((tm, D_MODEL), lambda i: (i, gd_blk + 1)),
            pl.BlockSpec((DIFF_WIDTH, D_MODEL), const),
            pl.BlockSpec((SB_WIDTH, D_MODEL), const),
            pl.BlockSpec((D_MODEL, D_MODEL), const),
        ],
        out_specs=pl.BlockSpec((tm, D_MODEL), lambda i: (i, 0)),
        out_shape=jax.ShapeDtypeStruct((t_rows, D_MODEL), F32),
        compiler_params=pltpu.CompilerParams(
            dimension_semantics=("arbitrary",), vmem_limit_bytes=VMEM_LIMIT),
        name="merge",
    )(x2d, o_diff, o_sb, proj2d, proj2d, w_o_diff, w_o_sb, w_out)


def _ffn_kernel(x_ref, g_ref, wi_ref, wd_ref, gf_ref, o_ref, *, final_norm):
    x = x_ref[...]
    ms = jnp.mean(x * x, axis=-1, keepdims=True)
    h = (x * lax.rsqrt(ms + NORM_EPS) * g_ref[...]).astype(BF16)
    y = x
    for c0 in range(0, FFN_HIDDEN, FFN_CH):
        c1 = min(c0 + FFN_CH, FFN_HIDDEN)
        gate = jnp.dot(h, wi_ref[:, c0:c1], preferred_element_type=F32)
        up = jnp.dot(h, wi_ref[:, FFN_HIDDEN + c0:FFN_HIDDEN + c1],
                     preferred_element_type=F32)
        act = (gate * jax.nn.sigmoid(gate) * up).astype(BF16)
        y = y + jnp.dot(act, wd_ref[c0:c1, :], preferred_element_type=F32)
    if final_norm:
        ms = jnp.mean(y * y, axis=-1, keepdims=True)
        y = y * lax.rsqrt(ms + NORM_EPS) * gf_ref[...]
    o_ref[...] = y


def _ffn(x2d, g, w_in, w_out, g_final, final_norm):
    t_rows = x2d.shape[0]
    tm = FFN_TM
    once = pl.Buffered(1)
    return pl.pallas_call(
        functools.partial(_ffn_kernel, final_norm=final_norm),
        grid=(t_rows // tm,),
        in_specs=[
            pl.BlockSpec((tm, D_MODEL), lambda i: (i, 0)),
            pl.BlockSpec((1, D_MODEL), lambda i: (0, 0)),
            pl.BlockSpec((D_MODEL, 2 * FFN_HIDDEN), lambda i: (0, 0), pipeline_mode=once),
            pl.BlockSpec((FFN_HIDDEN, D_MODEL), lambda i: (0, 0), pipeline_mode=once),
            pl.BlockSpec((1, D_MODEL), lambda i: (0, 0)),
        ],
        out_specs=pl.BlockSpec((tm, D_MODEL), lambda i: (i, 0)),
        out_shape=jax.ShapeDtypeStruct((t_rows, D_MODEL), F32),
        compiler_params=pltpu.CompilerParams(
            dimension_semantics=("arbitrary",), vmem_limit_bytes=VMEM_LIMIT),
        name="ffn",
    )(x2d, g, w_in, w_out, g_final)


def _rope_tables(seq):
    half = HEAD_DIM // 2
    pos = jnp.arange(seq, dtype=F32)
    inv = ROPE_THETA ** (-jnp.arange(0, HEAD_DIM, 2, dtype=F32) / HEAD_DIM)
    ang = pos[:, None] * inv[None, :]
    cos, sin = jnp.cos(ang), jnp.sin(ang)
    cos_t = jnp.tile(cos, (1, LANES // half))
    sin_t = jnp.tile(jnp.concatenate([-sin, sin], axis=1), (1, LANES // HEAD_DIM))
    return cos_t, sin_t


def kernel(x, norm_attn, w_in, b_gate, diff_lambda, diff_subln, w_o_diff, w_o_sb, w_out,
           norm_ffn, w_ffn_in, w_ffn_out, norm_final):
    b, s, d = x.shape
    cos_t, sin_t = _rope_tables(s)
    x2d = x.reshape(b * s, d)
    g_final = norm_final.reshape(1, d)
    for layer in range(DEPTH):
        lam_init = 0.8 - 0.6 * math.exp(-0.3 * layer)
        proj2d = _in_proj(x2d, norm_attn[layer].reshape(1, d), w_in[layer].astype(BF16),
                          b_gate[layer].reshape(1, 2 * d), cos_t, sin_t, s)
        proj = proj2d.reshape(b, s, IN_COLS)
        o_diff = _diff_attn(proj, diff_lambda[layer], diff_subln[layer].reshape(LANES, 1),
                            lam_init)
        o_sb = _sb_attn(proj)
        x2d = _merge(x2d, o_diff.reshape(b * s, DIFF_WIDTH), o_sb.reshape(b * s, SB_WIDTH),
                     proj2d, w_o_diff[layer].astype(BF16), w_o_sb[layer].astype(BF16),
                     w_out[layer].astype(BF16))
        x2d = _ffn(x2d, norm_ffn[layer].reshape(1, d), w_ffn_in[layer].astype(BF16),
                   w_ffn_out[layer].astype(BF16), g_final, layer == DEPTH - 1)
    return x2d.reshape(b, s, d)
```

```python
import functools
import math

import jax
import jax.numpy as jnp
from jax import lax
from jax.experimental import pallas as pl
from jax.experimental.pallas import tpu as pltpu

D_MODEL = 1024
DEPTH = 2
HEAD_DIM = 64
LANES = 128
DIFF_HEADS = 4
DIFF_WIDTH = DIFF_HEADS * 2 * HEAD_DIM
SB_HEADS = 8
SB_WIDTH = SB_HEADS * HEAD_DIM
SB_PAIRS = SB_WIDTH // LANES
IN_COLS = 3 * DIFF_WIDTH + 3 * SB_WIDTH + 2 * D_MODEL
GATE_COL0 = 3 * DIFF_WIDTH + 3 * SB_WIDTH
FFN_HIDDEN = 2816
ROPE_THETA = 10000.0
NORM_EPS = 1e-6
NEG_BIG = -1e30
LOG2E = math.log2(math.e)
SB_DEAD_LOG2 = -160.0

VMEM_LIMIT = 48 * 1024 * 1024

IN_TM, IN_CW = 512, 512
DIFF_TQ, DIFF_TK, DIFF_CC = 1024, 512, 256
SB_TQ, SB_TK, SB_CC = 512, 256, 256
MERGE_TM = 512
FFN_TM, FFN_CH = 512, 512

BF16 = jnp.bfloat16
F32 = jnp.float32


def _nt_dot(a, b):
    return lax.dot_general(a, b, (((1,), (1,)), ((), ())), preferred_element_type=F32)


def _in_proj_kernel(x_ref, g_ref, w_ref, bg_ref, cos_ref, sin_ref, o_ref):
    x = x_ref[...]
    ms = jnp.mean(x * x, axis=-1, keepdims=True)
    h = (x * lax.rsqrt(ms + NORM_EPS) * g_ref[...]).astype(BF16)

    scale = HEAD_DIM ** -0.5 * LOG2E
    cw = IN_CW
    cos = cos_ref[...]
    sin = sin_ref[...]
    lane = lax.broadcasted_iota(jnp.int32, cos.shape, 1)
    first_half = (lane % HEAD_DIM) < (HEAD_DIM // 2)

    for c0 in range(0, IN_COLS, cw):
        acc = jnp.dot(h, w_ref[:, c0:c0 + cw], preferred_element_type=F32)
        is_q = c0 < DIFF_WIDTH or 3 * DIFF_WIDTH <= c0 < 3 * DIFF_WIDTH + SB_WIDTH
        if c0 < 2 * DIFF_WIDTH:
            for b0 in range(0, cw, LANES):
                t = acc[:, b0:b0 + LANES]
                partner = jnp.where(first_half,
                                    pltpu.roll(t, LANES - HEAD_DIM // 2, 1),
                                    pltpu.roll(t, HEAD_DIM // 2, 1))
                r = t * cos + partner * sin
                if is_q:
                    r = r * scale
                o_ref[:, c0 + b0:c0 + b0 + LANES] = r.astype(BF16)
        elif c0 < GATE_COL0:
            if is_q:
                acc = acc * scale
            o_ref[:, c0:c0 + cw] = acc.astype(BF16)
        else:
            g0 = c0 - GATE_COL0
            o_ref[:, c0:c0 + cw] = jax.nn.sigmoid(acc + bg_ref[:, g0:g0 + cw]).astype(BF16)


def _in_proj(x2d, g, w, bg, cos_t, sin_t, seq):
    t_rows = x2d.shape[0]
    pos_tiles = seq // IN_TM
    return pl.pallas_call(
        _in_proj_kernel,
        grid=(t_rows // IN_TM,),
        in_specs=[
            pl.BlockSpec((IN_TM, D_MODEL), lambda i: (i, 0)),
            pl.BlockSpec((1, D_MODEL), lambda i: (0, 0)),
            pl.BlockSpec((D_MODEL, IN_COLS), lambda i: (0, 0)),
            pl.BlockSpec((1, 2 * D_MODEL), lambda i: (0, 0)),
            pl.BlockSpec((IN_TM, LANES), lambda i: (i % pos_tiles, 0)),
            pl.BlockSpec((IN_TM, LANES), lambda i: (i % pos_tiles, 0)),
        ],
        out_specs=pl.BlockSpec((IN_TM, IN_COLS), lambda i: (i, 0)),
        out_shape=jax.ShapeDtypeStruct((t_rows, IN_COLS), BF16),
        compiler_params=pltpu.CompilerParams(
            dimension_semantics=("arbitrary",), vmem_limit_bytes=VMEM_LIMIT),
        name="in_proj",
    )(x2d, g, w, bg, cos_t, sin_t)


def _tn_dot(a, b):
    return lax.dot_general(a, b, (((0,), (0,)), ((), ())), preferred_element_type=F32)


def _diff_attn_kernel(q_ref, k_ref, v_ref, lam_ref, subln_ref, o_ref, buf0, buf1, *,
                      lam_init):
    tq, tk, cc = DIFF_TQ, DIFF_TK, DIFF_CC
    qi = pl.program_id(2)
    q = q_ref[...]
    lane = lax.broadcasted_iota(jnp.int32, q.shape, 1)
    zero = jnp.zeros_like(q)
    qs = jnp.concatenate([jnp.where(lane < HEAD_DIM, q, zero),
                          jnp.where(lane >= HEAD_DIM, q, zero)], axis=0)
    chunks = [slice(c * cc, (c + 1) * cc) for c in range(2 * tq // cc)]
    key = lax.broadcasted_iota(jnp.int32, (tk, cc), 0)
    qry = lax.broadcasted_iota(jnp.int32, (tk, cc), 1)

    def scores(cols, k, diag):
        q0 = cols.start % tq
        if diag is not None and q0 + cc <= diag:
            return None
        s = _nt_dot(k, qs[cols])
        if diag is not None and q0 < diag + tk - 1:
            s = jnp.where((key + diag) <= (qry + q0), s, NEG_BIG)
        return s, jnp.max(s, axis=0, keepdims=True)

    def accumulate(s, tile_max, m, l, acc, v):
        m_new = jnp.maximum(m, tile_max)
        alpha = jnp.exp2(m - m_new)
        p = jnp.exp2(s - m_new)
        l = alpha * l + jnp.sum(p, axis=0, keepdims=True)
        acc = alpha * acc + _tn_dot(v, p.astype(BF16))
        return m_new, l, acc

    def step(kb_next, kb_pending, diag, buf_in, buf_out, carry):
        m, l, acc, tile_max = carry
        k = k_ref[pl.ds(pl.multiple_of(kb_next * tk, tk), tk), :]
        v = v_ref[pl.ds(pl.multiple_of(kb_pending * tk, tk), tk), :]
        m_n, l_n, acc_n, tile_max_n = [], [], [], []
        for c, cols in enumerate(chunks):
            scored = scores(cols, k, diag)
            if scored is not None:
                buf_out[:, cols] = scored[0]
            state = (m[c], l[c], acc[c])
            if tile_max[c] is not None:
                state = accumulate(buf_in[:, cols], tile_max[c], *state, v)
            m_n.append(state[0])
            l_n.append(state[1])
            acc_n.append(state[2])
            tile_max_n.append(None if scored is None else scored[1])
        return m_n, l_n, acc_n, tile_max_n

    per_q = tq // tk
    assert per_q == 2
    d0, d1 = qi * per_q, qi * per_q + 1
    kd1 = k_ref[pl.ds(pl.multiple_of(d1 * tk, tk), tk), :]
    tile_max = []
    for cols in chunks:
        scored = scores(cols, kd1, tk)
        if scored is not None:
            buf1[:, cols] = scored[0]
        tile_max.append(None if scored is None else scored[1])
    n_c = len(chunks)
    carry = ([jnp.full((1, cc), NEG_BIG, F32)] * n_c, [jnp.zeros((1, cc), F32)] * n_c,
             [jnp.zeros((LANES, cc), F32)] * n_c, tile_max)
    carry = step(d0, d1, 0, buf1, buf0, carry)

    def pair(j, carry):
        kb = d0 - 1 - 2 * j
        carry = step(kb, kb + 1, None, buf0, buf1, carry)
        return step(kb - 1, kb, None, buf1, buf0, carry)

    m, l, acc, tile_max = lax.fori_loop(0, qi, pair, carry)
    v_last = v_ref[pl.ds(0, tk), :]
    outs = [accumulate(buf0[:, cols], tile_max[c], m[c], l[c], acc[c], v_last)
            for c, cols in enumerate(chunks)]
    o = jnp.concatenate([oc[2] / oc[1] for oc in outs], axis=1)

    lp = lam_ref[...]
    e1 = jnp.exp(jnp.sum(lp[0:1, :] * lp[1:2, :], axis=1, keepdims=True))
    e2 = jnp.exp(jnp.sum(lp[2:3, :] * lp[3:4, :], axis=1, keepdims=True))
    lam = e1 - e2 + lam_init
    o = o[:, :tq] - lam * o[:, tq:]
    ms = jnp.mean(o * o, axis=0, keepdims=True)
    o = o * lax.rsqrt(ms + NORM_EPS) * subln_ref[...] * (1.0 - lam_init)
    o_ref[...] = o.T.astype(BF16)


def _diff_attn(proj, lam_p, subln, lam_init):
    b, s, _ = proj.shape
    nq = s // DIFF_TQ
    kcol0 = DIFF_WIDTH // LANES
    vcol0 = 2 * DIFF_WIDTH // LANES
    return pl.pallas_call(
        functools.partial(_diff_attn_kernel, lam_init=lam_init),
        grid=(b, DIFF_HEADS, nq),
        in_specs=[
            pl.BlockSpec((None, DIFF_TQ, LANES), lambda bi, h, qi: (bi, qi, h)),
            pl.BlockSpec((None, s, LANES), lambda bi, h, qi: (bi, 0, kcol0 + h)),
            pl.BlockSpec((None, s, LANES), lambda bi, h, qi: (bi, 0, vcol0 + h)),
            pl.BlockSpec((4, HEAD_DIM), lambda bi, h, qi: (0, 0)),
            pl.BlockSpec((LANES, 1), lambda bi, h, qi: (0, 0)),
        ],
        out_specs=pl.BlockSpec((None, DIFF_TQ, LANES), lambda bi, h, qi: (bi, qi, h)),
        out_shape=jax.ShapeDtypeStruct((b, s, DIFF_WIDTH), BF16),
        scratch_shapes=[pltpu.VMEM((DIFF_TK, 2 * DIFF_TQ), F32)] * 2,
        compiler_params=pltpu.CompilerParams(
            dimension_semantics=("arbitrary", "arbitrary", "arbitrary"),
            vmem_limit_bytes=VMEM_LIMIT),
        name="diff_attn",
    )(proj, proj, proj, lam_p, subln)


def _sb_attn_kernel(q_ref, k_ref, v_ref, o_ref, buf0, buf1):
    tq, tk, cc = SB_TQ, SB_TK, SB_CC
    qi = pl.program_id(2)
    q = q_ref[...]
    lane = lax.broadcasted_iota(jnp.int32, q.shape, 1)
    zero = jnp.zeros_like(q)
    per_q = tq // tk
    qs = jnp.concatenate([jnp.where(lane < HEAD_DIM, q, zero),
                          jnp.where(lane >= HEAD_DIM, q, zero)], axis=0)
    ss = lax.broadcasted_iota(jnp.int32, (tk, 2 * tk), 0)
    jj = lax.broadcasted_iota(jnp.int32, (tk, 2 * tk), 1) % tk
    neg_suffix = jnp.where(jj >= ss, -1.0, 0.0).astype(BF16)
    sign_bit = jnp.uint32(0x80000000)
    chunks = [slice(c * cc, (c + 1) * cc) for c in range(2 * tq // cc)]
    key = lax.broadcasted_iota(jnp.int32, (tk, cc), 0)
    qry = lax.broadcasted_iota(jnp.int32, (tk, cc), 1)

    def scores(cols, k, diag):
        q0 = cols.start % tq
        if diag is not None and q0 + cc <= diag + 1:
            return None, jnp.zeros((1, cc), F32)
        if diag is not None and q0 >= diag + tk:
            diag = None
        z = _nt_dot(k, qs[cols])
        neg_abs = lax.bitcast_convert_type(
            lax.bitcast_convert_type(z, jnp.uint32) | sign_bit, F32)
        sp = jnp.maximum(z, 0.0) + jnp.log2(1.0 + jnp.exp2(neg_abs))
        if diag is not None:
            mask = (key + diag) < (qry + q0)
            sp = jnp.where(mask, sp, 0.0)
        hi = sp.astype(BF16)
        lo = (sp - hi.astype(F32)).astype(BF16)
        arg = z + jnp.dot(neg_suffix, jnp.concatenate([hi, lo], axis=0),
                          preferred_element_type=F32)
        if diag is not None:
            arg = jnp.where(mask, arg, NEG_BIG)
        return arg, jnp.sum(sp, axis=0, keepdims=True)

    def accumulate(arg, rem, acc, v):
        a = jnp.exp2(arg + rem)
        return acc + _tn_dot(v, a.astype(BF16))

    def step(kb_next, kb_pending, diag, buf_in, buf_out, carry, pending_empty=(),
             exists=None):
        rem, acc, key_sum = carry
        k = k_ref[pl.ds(pl.multiple_of(kb_next * tk, tk), tk), :]
        v = v_ref[pl.ds(pl.multiple_of(kb_pending * tk, tk), tk), :]
        if exists is not None:
            exists_row = jnp.full((1, cc), exists, jnp.int32) > 0
        rem_n, acc_n, key_sum_n = [], [], []
        for c, cols in enumerate(chunks):
            arg_next, ks = scores(cols, k, diag)
            if exists is not None:
                arg_next = jnp.where(exists_row, arg_next, NEG_BIG)
                ks = jnp.where(exists_row, ks, 0.0)
            buf_out[:, cols] = arg_next
            acc_n.append(acc[c] if c in pending_empty
                         else accumulate(buf_in[:, cols], rem[c], acc[c], v))
            rem_n.append(rem[c] - key_sum[c])
            key_sum_n.append(ks)
        return rem_n, acc_n, key_sum_n

    def finish(buf, kb_pending, carry):
        rem, acc, _ = carry
        v = v_ref[pl.ds(pl.multiple_of(kb_pending * tk, tk), tk), :]
        return jnp.concatenate([accumulate(buf[:, cols], rem[c], acc[c], v)
                                for c, cols in enumerate(chunks)], axis=1)

    def next_live(carry):
        rem, _, key_sum = carry
        top = functools.reduce(jnp.maximum, [r - ks for r, ks in zip(rem, key_sum)])
        return (jnp.max(top) > SB_DEAD_LOG2).astype(jnp.int32)

    assert per_q == 2
    d1, d0 = qi * per_q + 1, qi * per_q
    f1 = jnp.maximum(d0 - 1, 0)
    k1 = k_ref[pl.ds(pl.multiple_of(d1 * tk, tk), tk), :]
    key_sum, empty = [], []
    for c, cols in enumerate(chunks):
        arg, ks = scores(cols, k1, tk)
        if arg is None:
            empty.append(c)
        else:
            buf0[:, cols] = arg
        key_sum.append(ks)
    carry = ([jnp.zeros((1, cc), F32)] * len(chunks),
             [jnp.zeros((LANES, cc), F32)] * len(chunks), key_sum)
    carry = step(d0, d1, 0, buf0, buf1, carry, pending_empty=empty)
    carry = step(f1, d0, None, buf1, buf0, carry, exists=(qi > 0).astype(jnp.int32))

    def pair(state):
        j, _, carry = state
        kb = f1 - 1 - 2 * j
        carry = step(kb, kb + 1, None, buf0, buf1, carry)
        carry = step(kb - 1, kb, None, buf1, buf0, carry)
        return j + 1, next_live(carry), carry

    n_pairs, go, carry = lax.while_loop(
        lambda st: (st[0] < qi - 1) & (st[1] > 0), pair,
        (jnp.int32(0), next_live(carry), carry))
    pending = f1 - 2 * n_pairs

    def leftover():
        return finish(buf1, 0, step(0, 1, None, buf0, buf1, carry))

    acc = lax.cond((pending == 1) & (go > 0), leftover, lambda: finish(buf0, pending, carry))
    feat = lax.broadcasted_iota(jnp.int32, (LANES, tq), 0)
    o_ref[...] = jnp.where(feat < HEAD_DIM, acc[:, :tq], acc[:, tq:]).T.astype(BF16)


def _sb_attn(proj):
    b, s, _ = proj.shape
    nq = s // SB_TQ
    qcol0 = 3 * DIFF_WIDTH // LANES
    kcol0 = qcol0 + SB_PAIRS
    vcol0 = kcol0 + SB_PAIRS
    return pl.pallas_call(
        _sb_attn_kernel,
        grid=(b, SB_PAIRS, nq),
        in_specs=[
            pl.BlockSpec((None, SB_TQ, LANES), lambda bi, p, qi: (bi, qi, qcol0 + p)),
            pl.BlockSpec((None, s, LANES), lambda bi, p, qi: (bi, 0, kcol0 + p)),
            pl.BlockSpec((None, s, LANES), lambda bi, p, qi: (bi, 0, vcol0 + p)),
        ],
        out_specs=pl.BlockSpec((None, SB_TQ, LANES), lambda bi, p, qi: (bi, qi, p)),
        out_shape=jax.ShapeDtypeStruct((b, s, SB_WIDTH), BF16),
        scratch_shapes=[pltpu.VMEM((SB_TK, 2 * SB_TQ), F32)] * 2,
        compiler_params=pltpu.CompilerParams(
            dimension_semantics=("arbitrary", "arbitrary", "arbitrary"),
            vmem_limit_bytes=VMEM_LIMIT),
        name="sb_attn",
    )(proj, proj, proj)


def _merge_kernel(x_ref, od_ref, os_ref, gd_ref, gs_ref, wd_ref, ws_ref, wo_ref, o_ref):
    pd = jnp.dot(od_ref[...], wd_ref[...], preferred_element_type=F32)
    ps = jnp.dot(os_ref[...], ws_ref[...], preferred_element_type=F32)
    merged = gd_ref[...].astype(F32) * pd + gs_ref[...].astype(F32) * ps
    o_ref[...] = x_ref[...] + jnp.dot(merged.astype(BF16), wo_ref[...],
                                      preferred_element_type=F32)


def _merge(x2d, o_diff, o_sb, proj2d, w_o_diff, w_o_sb, w_out):
    t_rows = x2d.shape[0]
    tm = MERGE_TM
    gd_blk = GATE_COL0 // D_MODEL
    const = lambda i: (0, 0)
    return pl.pallas_call(
        _merge_kernel,
        grid=(t_rows // tm,),
        in_specs=[
            pl.BlockSpec((tm, D_MODEL), lambda i: (i, 0)),
            pl.BlockSpec((tm, DIFF_WIDTH), lambda i: (i, 0)),
            pl.BlockSpec((tm, SB_WIDTH), lambda i: (i, 0)),
            pl.BlockSpec((tm, D_MODEL), lambda i: (i, gd_blk)),
            pl.BlockSpec((tm, D_MODEL), lambda i: (i, gd_blk + 1)),
            pl.BlockSpec((DIFF_WIDTH, D_MODEL), const),
            pl.BlockSpec((SB_WIDTH, D_MODEL), const),
            pl.BlockSpec((D_MODEL, D_MODEL), const),
        ],
        out_specs=pl.BlockSpec((tm, D_MODEL), lambda i: (i, 0)),
        out_shape=jax.ShapeDtypeStruct((t_rows, D_MODEL), F32),
        compiler_params=pltpu.CompilerParams(
            dimension_semantics=("arbitrary",), vmem_limit_bytes=VMEM_LIMIT),
        name="merge",
    )(x2d, o_diff, o_sb, proj2d, proj2d, w_o_diff, w_o_sb, w_out)


def _ffn_kernel(x_ref, g_ref, wi_ref, wd_ref, gf_ref, o_ref, *, final_norm):
    x = x_ref[...]
    ms = jnp.mean(x * x, axis=-1, keepdims=True)
    h = (x * lax.rsqrt(ms + NORM_EPS) * g_ref[...]).astype(BF16)
    y = x
    for c0 in range(0, FFN_HIDDEN, FFN_CH):
        c1 = min(c0 + FFN_CH, FFN_HIDDEN)
        gate = jnp.dot(h, wi_ref[:, c0:c1], preferred_element_type=F32)
        up = jnp.dot(h, wi_ref[:, FFN_HIDDEN + c0:FFN_HIDDEN + c1],
                     preferred_element_type=F32)
        act = (gate * jax.nn.sigmoid(gate) * up).astype(BF16)
        y = y + jnp.dot(act, wd_ref[c0:c1, :], preferred_element_type=F32)
    if final_norm:
        ms = jnp.mean(y * y, axis=-1, keepdims=True)
        y = y * lax.rsqrt(ms + NORM_EPS) * gf_ref[...]
    o_ref[...] = y


def _ffn(x2d, g, w_in, w_out, g_final, final_norm):
    t_rows = x2d.shape[0]
    tm = FFN_TM
    once = pl.Buffered(1)
    return pl.pallas_call(
        functools.partial(_ffn_kernel, final_norm=final_norm),
        grid=(t_rows // tm,),
        in_specs=[
            pl.BlockSpec((tm, D_MODEL), lambda i: (i, 0)),
            pl.BlockSpec((1, D_MODEL), lambda i: (0, 0)),
            pl.BlockSpec((D_MODEL, 2 * FFN_HIDDEN), lambda i: (0, 0), pipeline_mode=once),
            pl.BlockSpec((FFN_HIDDEN, D_MODEL), lambda i: (0, 0), pipeline_mode=once),
            pl.BlockSpec((1, D_MODEL), lambda i: (0, 0)),
        ],
        out_specs=pl.BlockSpec((tm, D_MODEL), lambda i: (i, 0)),
        out_shape=jax.ShapeDtypeStruct((t_rows, D_MODEL), F32),
        compiler_params=pltpu.CompilerParams(
            dimension_semantics=("arbitrary",), vmem_limit_bytes=VMEM_LIMIT),
        name="ffn",
    )(x2d, g, w_in, w_out, g_final)


def _rope_tables(seq):
    half = HEAD_DIM // 2
    pos = jnp.arange(seq, dtype=F32)
    inv = ROPE_THETA ** (-jnp.arange(0, HEAD_DIM, 2, dtype=F32) / HEAD_DIM)
    ang = pos[:, None] * inv[None, :]
    cos, sin = jnp.cos(ang), jnp.sin(ang)
    cos_t = jnp.tile(cos, (1, LANES // half))
    sin_t = jnp.tile(jnp.concatenate([-sin, sin], axis=1), (1, LANES // HEAD_DIM))
    return cos_t, sin_t


def kernel(x, norm_attn, w_in, b_gate, diff_lambda, diff_subln, w_o_diff, w_o_sb, w_out,
           norm_ffn, w_ffn_in, w_ffn_out, norm_final):
    b, s, d = x.shape
    cos_t, sin_t = _rope_tables(s)
    x2d = x.reshape(b * s, d)
    g_final = norm_final.reshape(1, d)
    for layer in range(DEPTH):
        lam_init = 0.8 - 0.6 * math.exp(-0.3 * layer)
        proj2d = _in_proj(x2d, norm_attn[layer].reshape(1, d), w_in[layer].astype(BF16),
                          b_gate[layer].reshape(1, 2 * d), cos_t, sin_t, s)
        proj = proj2d.reshape(b, s, IN_COLS)
        o_diff = _diff_attn(proj, diff_lambda[layer], diff_subln[layer].reshape(LANES, 1),
                            lam_init)
        o_sb = _sb_attn(proj)
        x2d = _merge(x2d, o_diff.reshape(b * s, DIFF_WIDTH), o_sb.reshape(b * s, SB_WIDTH),
                     proj2d, w_o_diff[layer].astype(BF16), w_o_sb[layer].astype(BF16),
                     w_out[layer].astype(BF16))
        x2d = _ffn(x2d, norm_ffn[layer].reshape(1, d), w_ffn_in[layer].astype(BF16),
                   w_ffn_out[layer].astype(BF16), g_final, layer == DEPTH - 1)
    return x2d.reshape(b, s, d)
```

```python
import functools
import math

import jax
import jax.numpy as jnp
from jax import lax
from jax.experimental import pallas as pl
from jax.experimental.pallas import tpu as pltpu

D_MODEL = 1024
DEPTH = 2
HEAD_DIM = 64
LANES = 128
DIFF_HEADS = 4
DIFF_WIDTH = DIFF_HEADS * 2 * HEAD_DIM
SB_HEADS = 8
SB_WIDTH = SB_HEADS * HEAD_DIM
SB_PAIRS = SB_WIDTH // LANES
IN_COLS = 3 * DIFF_WIDTH + 3 * SB_WIDTH + 2 * D_MODEL
GATE_COL0 = 3 * DIFF_WIDTH + 3 * SB_WIDTH
FFN_HIDDEN = 2816
ROPE_THETA = 10000.0
NORM_EPS = 1e-6
NEG_BIG = -1e30
LOG2E = math.log2(math.e)
SB_DEAD_LOG2 = -160.0

VMEM_LIMIT = 48 * 1024 * 1024

IN_TM, IN_CW = 512, 512
DIFF_TQ, DIFF_TK, DIFF_CC = 1024, 512, 256
SB_TQ, SB_TK, SB_CC = 512, 256, 256
MERGE_TM = 512
FFN_TM, FFN_CH = 512, 512

BF16 = jnp.bfloat16
F32 = jnp.float32


def _nt_dot(a, b):
    return lax.dot_general(a, b, (((1,), (1,)), ((), ())), preferred_element_type=F32)


def _in_proj_kernel(x_ref, g_ref, w_ref, bg_ref, cos_ref, sin_ref, o_ref):
    x = x_ref[...]
    ms = jnp.mean(x * x, axis=-1, keepdims=True)
    h = (x * lax.rsqrt(ms + NORM_EPS) * g_ref[...]).astype(BF16)

    scale = HEAD_DIM ** -0.5 * LOG2E
    cw = IN_CW
    cos = cos_ref[...]
    sin = sin_ref[...]
    lane = lax.broadcasted_iota(jnp.int32, cos.shape, 1)
    first_half = (lane % HEAD_DIM) < (HEAD_DIM // 2)

    for c0 in range(0, IN_COLS, cw):
        acc = jnp.dot(h, w_ref[:, c0:c0 + cw], preferred_element_type=F32)
        is_q = c0 < DIFF_WIDTH or 3 * DIFF_WIDTH <= c0 < 3 * DIFF_WIDTH + SB_WIDTH
        if c0 < 2 * DIFF_WIDTH:
            for b0 in range(0, cw, LANES):
                t = acc[:, b0:b0 + LANES]
                partner = jnp.where(first_half,
                                    pltpu.roll(t, LANES - HEAD_DIM // 2, 1),
                                    pltpu.roll(t, HEAD_DIM // 2, 1))
                r = t * cos + partner * sin
                if is_q:
                    r = r * scale
                o_ref[:, c0 + b0:c0 + b0 + LANES] = r.astype(BF16)
        elif c0 < GATE_COL0:
            if is_q:
                acc = acc * scale
            o_ref[:, c0:c0 + cw] = acc.astype(BF16)
        else:
            g0 = c0 - GATE_COL0
            o_ref[:, c0:c0 + cw] = jax.nn.sigmoid(acc + bg_ref[:, g0:g0 + cw]).astype(BF16)


def _in_proj(x2d, g, w, bg, cos_t, sin_t, seq):
    t_rows = x2d.shape[0]
    pos_tiles = seq // IN_TM
    return pl.pallas_call(
        _in_proj_kernel,
        grid=(t_rows // IN_TM,),
        in_specs=[
            pl.BlockSpec((IN_TM, D_MODEL), lambda i: (i, 0)),
            pl.BlockSpec((1, D_MODEL), lambda i: (0, 0)),
            pl.BlockSpec((D_MODEL, IN_COLS), lambda i: (0, 0)),
            pl.BlockSpec((1, 2 * D_MODEL), lambda i: (0, 0)),
            pl.BlockSpec((IN_TM, LANES), lambda i: (i % pos_tiles, 0)),
            pl.BlockSpec((IN_TM, LANES), lambda i: (i % pos_tiles, 0)),
        ],
        out_specs=pl.BlockSpec((IN_TM, IN_COLS), lambda i: (i, 0)),
        out_shape=jax.ShapeDtypeStruct((t_rows, IN_COLS), BF16),
        compiler_params=pltpu.CompilerParams(
            dimension_semantics=("arbitrary",), vmem_limit_bytes=VMEM_LIMIT),
        name="in_proj",
    )(x2d, g, w, bg, cos_t, sin_t)


def _tn_dot(a, b):
    return lax.dot_general(a, b, (((0,), (0,)), ((), ())), preferred_element_type=F32)


def _diff_attn_kernel(q_ref, k_ref, v_ref, lam_ref, subln_ref, o_ref, buf0, buf1, *,
                      lam_init):
    tq = DIFF_TQ

    def q_tile(qi, carry):
        rows = pl.ds(pl.multiple_of(qi * tq, tq), tq)
        o_ref[rows, :] = _diff_q_tile(qi, q_ref[rows, :], k_ref, v_ref, lam_ref, subln_ref,
                                      buf0, buf1, lam_init)
        return carry

    lax.fori_loop(0, q_ref.shape[0] // tq, q_tile, 0)


def _diff_q_tile(qi, q, k_ref, v_ref, lam_ref, subln_ref, buf0, buf1, lam_init):
    tq, tk, cc = DIFF_TQ, DIFF_TK, DIFF_CC
    lane = lax.broadcasted_iota(jnp.int32, q.shape, 1)
    zero = jnp.zeros_like(q)
    qs = jnp.concatenate([jnp.where(lane < HEAD_DIM, q, zero),
                          jnp.where(lane >= HEAD_DIM, q, zero)], axis=0)
    chunks = [slice(c * cc, (c + 1) * cc) for c in range(2 * tq // cc)]
    key = lax.broadcasted_iota(jnp.int32, (tk, cc), 0)
    qry = lax.broadcasted_iota(jnp.int32, (tk, cc), 1)

    def scores(cols, k, diag):
        q0 = cols.start % tq
        if diag is not None and q0 + cc <= diag:
            return None
        s = _nt_dot(k, qs[cols])
        if diag is not None and q0 < diag + tk - 1:
            s = jnp.where((key + diag) <= (qry + q0), s, NEG_BIG)
        return s, jnp.max(s, axis=0, keepdims=True)

    def accumulate(s, tile_max, m, l, acc, v):
        m_new = jnp.maximum(m, tile_max)
        alpha = jnp.exp2(m - m_new)
        p = jnp.exp2(s - m_new)
        l = alpha * l + jnp.sum(p, axis=0, keepdims=True)
        acc = alpha * acc + _tn_dot(v, p.astype(BF16))
        return m_new, l, acc

    def step(kb_next, kb_pending, diag, buf_in, buf_out, carry):
        m, l, acc, tile_max = carry
        k = k_ref[pl.ds(pl.multiple_of(kb_next * tk, tk), tk), :]
        v = v_ref[pl.ds(pl.multiple_of(kb_pending * tk, tk), tk), :]
        m_n, l_n, acc_n, tile_max_n = [], [], [], []
        for c, cols in enumerate(chunks):
            scored = scores(cols, k, diag)
            if scored is not None:
                buf_out[:, cols] = scored[0]
            state = (m[c], l[c], acc[c])
            if tile_max[c] is not None:
                state = accumulate(buf_in[:, cols], tile_max[c], *state, v)
            m_n.append(state[0])
            l_n.append(state[1])
            acc_n.append(state[2])
            tile_max_n.append(None if scored is None else scored[1])
        return m_n, l_n, acc_n, tile_max_n

    per_q = tq // tk
    assert per_q == 2
    d0, d1 = qi * per_q, qi * per_q + 1
    kd1 = k_ref[pl.ds(pl.multiple_of(d1 * tk, tk), tk), :]
    tile_max = []
    for cols in chunks:
        scored = scores(cols, kd1, tk)
        if scored is not None:
            buf1[:, cols] = scored[0]
        tile_max.append(None if scored is None else scored[1])
    n_c = len(chunks)
    carry = ([jnp.full((1, cc), NEG_BIG, F32)] * n_c, [jnp.zeros((1, cc), F32)] * n_c,
             [jnp.zeros((LANES, cc), F32)] * n_c, tile_max)
    carry = step(d0, d1, 0, buf1, buf0, carry)

    def pair(j, carry):
        kb = d0 - 1 - 2 * j
        carry = step(kb, kb + 1, None, buf0, buf1, carry)
        return step(kb - 1, kb, None, buf1, buf0, carry)

    m, l, acc, tile_max = lax.fori_loop(0, qi, pair, carry)
    v_last = v_ref[pl.ds(0, tk), :]
    outs = [accumulate(buf0[:, cols], tile_max[c], m[c], l[c], acc[c], v_last)
            for c, cols in enumerate(chunks)]
    o = jnp.concatenate([oc[2] / oc[1] for oc in outs], axis=1)

    lp = lam_ref[...]
    e1 = jnp.exp(jnp.sum(lp[0:1, :] * lp[1:2, :], axis=1, keepdims=True))
    e2 = jnp.exp(jnp.sum(lp[2:3, :] * lp[3:4, :], axis=1, keepdims=True))
    lam = e1 - e2 + lam_init
    o = o[:, :tq] - lam * o[:, tq:]
    ms = jnp.mean(o * o, axis=0, keepdims=True)
    o = o * lax.rsqrt(ms + NORM_EPS) * subln_ref[...] * (1.0 - lam_init)
    return o.T.astype(BF16)


def _diff_attn(proj, lam_p, subln, lam_init):
    b, s, _ = proj.shape
    kcol0 = DIFF_WIDTH // LANES
    vcol0 = 2 * DIFF_WIDTH // LANES
    return pl.pallas_call(
        functools.partial(_diff_attn_kernel, lam_init=lam_init),
        grid=(b, DIFF_HEADS),
        in_specs=[
            pl.BlockSpec((None, s, LANES), lambda bi, h: (bi, 0, h)),
            pl.BlockSpec((None, s, LANES), lambda bi, h: (bi, 0, kcol0 + h)),
            pl.BlockSpec((None, s, LANES), lambda bi, h: (bi, 0, vcol0 + h)),
            pl.BlockSpec((4, HEAD_DIM), lambda bi, h: (0, 0)),
            pl.BlockSpec((LANES, 1), lambda bi, h: (0, 0)),
        ],
        out_specs=pl.BlockSpec((None, s, LANES), lambda bi, h: (bi, 0, h)),
        out_shape=jax.ShapeDtypeStruct((b, s, DIFF_WIDTH), BF16),
        scratch_shapes=[pltpu.VMEM((DIFF_TK, 2 * DIFF_TQ), F32)] * 2,
        compiler_params=pltpu.CompilerParams(
            dimension_semantics=("arbitrary", "arbitrary"), vmem_limit_bytes=VMEM_LIMIT),
        name="diff_attn",
    )(proj, proj, proj, lam_p, subln)


def _sb_attn_kernel(q_ref, k_ref, v_ref, o_ref, buf0, buf1):
    tq = SB_TQ

    def q_tile(qi, carry):
        rows = pl.ds(pl.multiple_of(qi * tq, tq), tq)
        o_ref[rows, :] = _sb_q_tile(qi, q_ref[rows, :], k_ref, v_ref, buf0, buf1)
        return carry

    lax.fori_loop(0, q_ref.shape[0] // tq, q_tile, 0)


def _sb_q_tile(qi, q, k_ref, v_ref, buf0, buf1):
    tq, tk, cc = SB_TQ, SB_TK, SB_CC
    lane = lax.broadcasted_iota(jnp.int32, q.shape, 1)
    zero = jnp.zeros_like(q)
    per_q = tq // tk
    qs = jnp.concatenate([jnp.where(lane < HEAD_DIM, q, zero),
                          jnp.where(lane >= HEAD_DIM, q, zero)], axis=0)
    ss = lax.broadcasted_iota(jnp.int32, (tk, 2 * tk), 0)
    jj = lax.broadcasted_iota(jnp.int32, (tk, 2 * tk), 1) % tk
    neg_suffix = jnp.where(jj >= ss, -1.0, 0.0).astype(BF16)
    sign_bit = jnp.uint32(0x80000000)
    chunks = [slice(c * cc, (c + 1) * cc) for c in range(2 * tq // cc)]
    key = lax.broadcasted_iota(jnp.int32, (tk, cc), 0)
    qry = lax.broadcasted_iota(jnp.int32, (tk, cc), 1)

    def scores(cols, k, diag):
        q0 = cols.start % tq
        if diag is not None and q0 + cc <= diag + 1:
            return None, jnp.zeros((1, cc), F32)
        if diag is not None and q0 >= diag + tk:
            diag = None
        z = _nt_dot(k, qs[cols])
        neg_abs = lax.bitcast_convert_type(
            lax.bitcast_convert_type(z, jnp.uint32) | sign_bit, F32)
        sp = jnp.maximum(z, 0.0) + jnp.log2(1.0 + jnp.exp2(neg_abs))
        if diag is not None:
            mask = (key + diag) < (qry + q0)
            sp = jnp.where(mask, sp, 0.0)
        hi = sp.astype(BF16)
        lo = (sp - hi.astype(F32)).astype(BF16)
        arg = z + jnp.dot(neg_suffix, jnp.concatenate([hi, lo], axis=0),
                          preferred_element_type=F32)
        if diag is not None:
            arg = jnp.where(mask, arg, NEG_BIG)
        return arg, jnp.sum(sp, axis=0, keepdims=True)

    def accumulate(arg, rem, acc, v):
        a = jnp.exp2(arg + rem)
        return acc + _tn_dot(v, a.astype(BF16))

    def step(kb_next, kb_pending, diag, buf_in, buf_out, carry, pending_empty=(),
             exists=None):
        rem, acc, key_sum = carry
        k = k_ref[pl.ds(pl.multiple_of(kb_next * tk, tk), tk), :]
        v = v_ref[pl.ds(pl.multiple_of(kb_pending * tk, tk), tk), :]
        if exists is not None:
            exists_row = jnp.full((1, cc), exists, jnp.int32) > 0
        rem_n, acc_n, key_sum_n = [], [], []
        for c, cols in enumerate(chunks):
            arg_next, ks = scores(cols, k, diag)
            if exists is not None:
                arg_next = jnp.where(exists_row, arg_next, NEG_BIG)
                ks = jnp.where(exists_row, ks, 0.0)
            buf_out[:, cols] = arg_next
            acc_n.append(acc[c] if c in pending_empty
                         else accumulate(buf_in[:, cols], rem[c], acc[c], v))
            rem_n.append(rem[c] - key_sum[c])
            key_sum_n.append(ks)
        return rem_n, acc_n, key_sum_n

    def finish(buf, kb_pending, carry):
        rem, acc, _ = carry
        v = v_ref[pl.ds(pl.multiple_of(kb_pending * tk, tk), tk), :]
        return jnp.concatenate([accumulate(buf[:, cols], rem[c], acc[c], v)
                                for c, cols in enumerate(chunks)], axis=1)

    def next_live(carry):
        rem, _, key_sum = carry
        top = functools.reduce(jnp.maximum, [r - ks for r, ks in zip(rem, key_sum)])
        return (jnp.max(top) > SB_DEAD_LOG2).astype(jnp.int32)

    assert per_q == 2
    d1, d0 = qi * per_q + 1, qi * per_q
    f1 = jnp.maximum(d0 - 1, 0)
    k1 = k_ref[pl.ds(pl.multiple_of(d1 * tk, tk), tk), :]
    key_sum, empty = [], []
    for c, cols in enumerate(chunks):
        arg, ks = scores(cols, k1, tk)
        if arg is None:
            empty.append(c)
        else:
            buf0[:, cols] = arg
        key_sum.append(ks)
    carry = ([jnp.zeros((1, cc), F32)] * len(chunks),
             [jnp.zeros((LANES, cc), F32)] * len(chunks), key_sum)
    carry = step(d0, d1, 0, buf0, buf1, carry, pending_empty=empty)
    carry = step(f1, d0, None, buf1, buf0, carry, exists=(qi > 0).astype(jnp.int32))

    def pair(state):
        j, _, carry = state
        kb = f1 - 1 - 2 * j
        carry = step(kb, kb + 1, None, buf0, buf1, carry)
        carry = step(kb - 1, kb, None, buf1, buf0, carry)
        return j + 1, next_live(carry), carry

    n_pairs, go, carry = lax.while_loop(
        lambda st: (st[0] < qi - 1) & (st[1] > 0), pair,
        (jnp.int32(0), next_live(carry), carry))
    pending = f1 - 2 * n_pairs

    def leftover():
        return finish(buf1, 0, step(0, 1, None, buf0, buf1, carry))

    acc = lax.cond((pending == 1) & (go > 0), leftover, lambda: finish(buf0, pending, carry))
    feat = lax.broadcasted_iota(jnp.int32, (LANES, tq), 0)
    return jnp.where(feat < HEAD_DIM, acc[:, :tq], acc[:, tq:]).T.astype(BF16)


def _sb_attn(proj):
    b, s, _ = proj.shape
    qcol0 = 3 * DIFF_WIDTH // LANES
    kcol0 = qcol0 + SB_PAIRS
    vcol0 = kcol0 + SB_PAIRS
    return pl.pallas_call(
        _sb_attn_kernel,
        grid=(b, SB_PAIRS),
        in_specs=[
            pl.BlockSpec((None, s, LANES), lambda bi, p: (bi, 0, qcol0 + p)),
            pl.BlockSpec((None, s, LANES), lambda bi, p: (bi, 0, kcol0 + p)),
            pl.BlockSpec((None, s, LANES), lambda bi, p: (bi, 0, vcol0 + p)),
        ],
        out_specs=pl.BlockSpec((None, s, LANES), lambda bi, p: (bi, 0, p)),
        out_shape=jax.ShapeDtypeStruct((b, s, SB_WIDTH), BF16),
        scratch_shapes=[pltpu.VMEM((SB_TK, 2 * SB_TQ), F32)] * 2,
        compiler_params=pltpu.CompilerParams(
            dimension_semantics=("arbitrary", "arbitrary"), vmem_limit_bytes=VMEM_LIMIT),
        name="sb_attn",
    )(proj, proj, proj)


def _merge_kernel(x_ref, od_ref, os_ref, gd_ref, gs_ref, wd_ref, ws_ref, wo_ref, o_ref):
    pd = jnp.dot(od_ref[...], wd_ref[...], preferred_element_type=F32)
    ps = jnp.dot(os_ref[...], ws_ref[...], preferred_element_type=F32)
    merged = gd_ref[...].astype(F32) * pd + gs_ref[...].astype(F32) * ps
    o_ref[...] = x_ref[...] + jnp.dot(merged.astype(BF16), wo_ref[...],
                                      preferred_element_type=F32)


def _merge(x2d, o_diff, o_sb, proj2d, w_o_diff, w_o_sb, w_out):
    t_rows = x2d.shape[0]
    tm = MERGE_TM
    gd_blk = GATE_COL0 // D_MODEL
    const = lambda i: (0, 0)
    return pl.pallas_call(
        _merge_kernel,
        grid=(t_rows // tm,),
        in_specs=[
            pl.BlockSpec((tm, D_MODEL), lambda i: (i, 0)),
            pl.BlockSpec((tm, DIFF_WIDTH), lambda i: (i, 0)),
            pl.BlockSpec((tm, SB_WIDTH), lambda i: (i, 0)),
            pl.BlockSpec((tm, D_MODEL), lambda i: (i, gd_blk)),
            pl.BlockSpec((tm, D_MODEL), lambda i: (i, gd_blk + 1)),
            pl.BlockSpec((DIFF_WIDTH, D_MODEL), const),
            pl.BlockSpec((SB_WIDTH, D_MODEL), const),
            pl.BlockSpec((D_MODEL, D_MODEL), const),
        ],
        out_specs=pl.BlockSpec((tm, D_MODEL), lambda i: (i, 0)),
        out_shape=jax.ShapeDtypeStruct((t_rows, D_MODEL), F32),
        compiler_params=pltpu.CompilerParams(
            dimension_semantics=("arbitrary",), vmem_limit_bytes=VMEM_LIMIT),
        name="merge",
    )(x2d, o_diff, o_sb, proj2d, proj2d, w_o_diff, w_o_sb, w_out)


def _ffn_kernel(x_ref, g_ref, wi_ref, wd_ref, gf_ref, o_ref, *, final_norm):
    x = x_ref[...]
    ms = jnp.mean(x * x, axis=-1, keepdims=True)
    h = (x * lax.rsqrt(ms + NORM_EPS) * g_ref[...]).astype(BF16)
    y = x
    for c0 in range(0, FFN_HIDDEN, FFN_CH):
        c1 = min(c0 + FFN_CH, FFN_HIDDEN)
        gate = jnp.dot(h, wi_ref[:, c0:c1], preferred_element_type=F32)
        up = jnp.dot(h, wi_ref[:, FFN_HIDDEN + c0:FFN_HIDDEN + c1],
                     preferred_element_type=F32)
        act = (gate * jax.nn.sigmoid(gate) * up).astype(BF16)
        y = y + jnp.dot(act, wd_ref[c0:c1, :], preferred_element_type=F32)
    if final_norm:
        ms = jnp.mean(y * y, axis=-1, keepdims=True)
        y = y * lax.rsqrt(ms + NORM_EPS) * gf_ref[...]
    o_ref[...] = y


def _ffn(x2d, g, w_in, w_out, g_final, final_norm):
    t_rows = x2d.shape[0]
    tm = FFN_TM
    once = pl.Buffered(1)
    return pl.pallas_call(
        functools.partial(_ffn_kernel, final_norm=final_norm),
        grid=(t_rows // tm,),
        in_specs=[
            pl.BlockSpec((tm, D_MODEL), lambda i: (i, 0)),
            pl.BlockSpec((1, D_MODEL), lambda i: (0, 0)),
            pl.BlockSpec((D_MODEL, 2 * FFN_HIDDEN), lambda i: (0, 0), pipeline_mode=once),
            pl.BlockSpec((FFN_HIDDEN, D_MODEL), lambda i: (0, 0), pipeline_mode=once),
            pl.BlockSpec((1, D_MODEL), lambda i: (0, 0)),
        ],
        out_specs=pl.BlockSpec((tm, D_MODEL), lambda i: (i, 0)),
        out_shape=jax.ShapeDtypeStruct((t_rows, D_MODEL), F32),
        compiler_params=pltpu.CompilerParams(
            dimension_semantics=("arbitrary",), vmem_limit_bytes=VMEM_LIMIT),
        name="ffn",
    )(x2d, g, w_in, w_out, g_final)


def _rope_tables(seq):
    half = HEAD_DIM // 2
    pos = jnp.arange(seq, dtype=F32)
    inv = ROPE_THETA ** (-jnp.arange(0, HEAD_DIM, 2, dtype=F32) / HEAD_DIM)
    ang = pos[:, None] * inv[None, :]
    cos, sin = jnp.cos(ang), jnp.sin(ang)
    cos_t = jnp.tile(cos, (1, LANES // half))
    sin_t = jnp.tile(jnp.concatenate([-sin, sin], axis=1), (1, LANES // HEAD_DIM))
    return cos_t, sin_t


def kernel(x, norm_attn, w_in, b_gate, diff_lambda, diff_subln, w_o_diff, w_o_sb, w_out,
           norm_ffn, w_ffn_in, w_ffn_out, norm_final):
    b, s, d = x.shape
    cos_t, sin_t = _rope_tables(s)
    x2d = x.reshape(b * s, d)
    g_final = norm_final.reshape(1, d)
    for layer in range(DEPTH):
        lam_init = 0.8 - 0.6 * math.exp(-0.3 * layer)
        proj2d = _in_proj(x2d, norm_attn[layer].reshape(1, d), w_in[layer].astype(BF16),
                          b_gate[layer].reshape(1, 2 * d), cos_t, sin_t, s)
        proj = proj2d.reshape(b, s, IN_COLS)
        o_diff = _diff_attn(proj, diff_lambda[layer], diff_subln[layer].reshape(LANES, 1),
                            lam_init)
        o_sb = _sb_attn(proj)
        x2d = _merge(x2d, o_diff.reshape(b * s, DIFF_WIDTH), o_sb.reshape(b * s, SB_WIDTH),
                     proj2d, w_o_diff[layer].astype(BF16), w_o_sb[layer].astype(BF16),
                     w_out[layer].astype(BF16))
        x2d = _ffn(x2d, norm_ffn[layer].reshape(1, d), w_ffn_in[layer].astype(BF16),
                   w_ffn_out[layer].astype(BF16), g_final, layer == DEPTH - 1)
    return x2d.reshape(b, s, d)
```

```python
import functools
import math

import jax
import jax.numpy as jnp
from jax import lax
from jax.experimental import pallas as pl
from jax.experimental.pallas import tpu as pltpu

D_MODEL = 1024
DEPTH = 2
HEAD_DIM = 64
LANES = 128
DIFF_HEADS = 4
DIFF_WIDTH = DIFF_HEADS * 2 * HEAD_DIM
SB_HEADS = 8
SB_WIDTH = SB_HEADS * HEAD_DIM
SB_PAIRS = SB_WIDTH // LANES
IN_COLS = 3 * DIFF_WIDTH + 3 * SB_WIDTH + 2 * D_MODEL
GATE_COL0 = 3 * DIFF_WIDTH + 3 * SB_WIDTH
FFN_HIDDEN = 2816
ROPE_THETA = 10000.0
NORM_EPS = 1e-6
NEG_BIG = -1e30
LOG2E = math.log2(math.e)
SB_DEAD_LOG2 = -160.0

VMEM_LIMIT = 48 * 1024 * 1024

IN_TM, IN_CW = 512, 512
DIFF_TQ, DIFF_TK, DIFF_CC = 1024, 512, 256
SB_TQ, SB_TK, SB_CC = 512, 256, 256
FFN_TM, FFN_CH = 512, 512

BF16 = jnp.bfloat16
F32 = jnp.float32


def _nt_dot(a, b):
    return lax.dot_general(a, b, (((1,), (1,)), ((), ())), preferred_element_type=F32)


def _in_proj_kernel(x_ref, g_ref, w_ref, bg_ref, cos_ref, sin_ref, o_ref):
    x = x_ref[...]
    ms = jnp.mean(x * x, axis=-1, keepdims=True)
    h = (x * lax.rsqrt(ms + NORM_EPS) * g_ref[...]).astype(BF16)

    scale = HEAD_DIM ** -0.5 * LOG2E
    cw = IN_CW
    assert DIFF_WIDTH % cw == 0 and SB_WIDTH % cw == 0
    cos = cos_ref[...]
    sin = sin_ref[...]
    lane = lax.broadcasted_iota(jnp.int32, cos.shape, 1)
    first_half = (lane % HEAD_DIM) < (HEAD_DIM // 2)

    for c0 in range(0, IN_COLS, cw):
        acc = jnp.dot(h, w_ref[:, c0:c0 + cw], preferred_element_type=F32)
        is_q = c0 < DIFF_WIDTH or 3 * DIFF_WIDTH <= c0 < 3 * DIFF_WIDTH + SB_WIDTH
        if c0 < 2 * DIFF_WIDTH:
            for b0 in range(0, cw, LANES):
                t = acc[:, b0:b0 + LANES]
                partner = jnp.where(first_half,
                                    pltpu.roll(t, LANES - HEAD_DIM // 2, 1),
                                    pltpu.roll(t, HEAD_DIM // 2, 1))
                r = t * cos + partner * sin
                if is_q:
                    r = r * scale
                o_ref[:, c0 + b0:c0 + b0 + LANES] = r.astype(BF16)
        elif c0 < GATE_COL0:
            if is_q:
                acc = acc * scale
            o_ref[:, c0:c0 + cw] = acc.astype(BF16)
        else:
            g0 = c0 - GATE_COL0
            o_ref[:, c0:c0 + cw] = jax.nn.sigmoid(acc + bg_ref[:, g0:g0 + cw]).astype(BF16)


def _in_proj(x2d, g, w, bg, cos_t, sin_t, seq):
    t_rows = x2d.shape[0]
    pos_tiles = seq // IN_TM
    return pl.pallas_call(
        _in_proj_kernel,
        grid=(t_rows // IN_TM,),
        in_specs=[
            pl.BlockSpec((IN_TM, D_MODEL), lambda i: (i, 0)),
            pl.BlockSpec((1, D_MODEL), lambda i: (0, 0)),
            pl.BlockSpec((D_MODEL, IN_COLS), lambda i: (0, 0)),
            pl.BlockSpec((1, 2 * D_MODEL), lambda i: (0, 0)),
            pl.BlockSpec((IN_TM, LANES), lambda i: (i % pos_tiles, 0)),
            pl.BlockSpec((IN_TM, LANES), lambda i: (i % pos_tiles, 0)),
        ],
        out_specs=pl.BlockSpec((IN_TM, IN_COLS), lambda i: (i, 0)),
        out_shape=jax.ShapeDtypeStruct((t_rows, IN_COLS), BF16),
        compiler_params=pltpu.CompilerParams(
            dimension_semantics=("arbitrary",), vmem_limit_bytes=VMEM_LIMIT),
        name="in_proj",
    )(x2d, g, w, bg, cos_t, sin_t)


def _tn_dot(a, b):
    return lax.dot_general(a, b, (((0,), (0,)), ((), ())), preferred_element_type=F32)


def _diff_attn_kernel(q_ref, k_ref, v_ref, lam_ref, subln_ref, o_ref, buf0, buf1, *,
                      lam_init):
    tq = DIFF_TQ

    def q_tile(qi, carry):
        rows = pl.ds(pl.multiple_of(qi * tq, tq), tq)
        o_ref[rows, :] = _diff_q_tile(qi, q_ref[rows, :], k_ref, v_ref, lam_ref, subln_ref,
                                      buf0, buf1, lam_init)
        return carry

    lax.fori_loop(0, q_ref.shape[0] // tq, q_tile, 0)


def _diff_q_tile(qi, q, k_ref, v_ref, lam_ref, subln_ref, buf0, buf1, lam_init):
    tq, tk, cc = DIFF_TQ, DIFF_TK, DIFF_CC
    lane = lax.broadcasted_iota(jnp.int32, q.shape, 1)
    zero = jnp.zeros_like(q)
    qs = jnp.concatenate([jnp.where(lane < HEAD_DIM, q, zero),
                          jnp.where(lane >= HEAD_DIM, q, zero)], axis=0)
    chunks = [slice(c * cc, (c + 1) * cc) for c in range(2 * tq // cc)]
    key = lax.broadcasted_iota(jnp.int32, (tk, cc), 0)
    qry = lax.broadcasted_iota(jnp.int32, (tk, cc), 1)

    def scores(cols, k, diag):
        q0 = cols.start % tq
        if diag is not None and q0 + cc <= diag:
            return None
        s = _nt_dot(k, qs[cols])
        if diag is not None and q0 < diag + tk - 1:
            s = jnp.where((key + diag) <= (qry + q0), s, NEG_BIG)
        return s, jnp.max(s, axis=0, keepdims=True)

    def accumulate(s, tile_max, m, l, acc, v):
        m_new = jnp.maximum(m, tile_max)
        alpha = jnp.exp2(m - m_new)
        p = jnp.exp2(s - m_new)
        l = alpha * l + jnp.sum(p, axis=0, keepdims=True)
        acc = alpha * acc + _tn_dot(v, p.astype(BF16))
        return m_new, l, acc

    def step(kb_next, kb_pending, diag, buf_in, buf_out, carry):
        m, l, acc, tile_max = carry
        k = k_ref[pl.ds(pl.multiple_of(kb_next * tk, tk), tk), :]
        v = v_ref[pl.ds(pl.multiple_of(kb_pending * tk, tk), tk), :]
        m_n, l_n, acc_n, tile_max_n = [], [], [], []
        for c, cols in enumerate(chunks):
            scored = scores(cols, k, diag)
            if scored is not None:
                buf_out[:, cols] = scored[0]
            state = (m[c], l[c], acc[c])
            if tile_max[c] is not None:
                state = accumulate(buf_in[:, cols], tile_max[c], *state, v)
            m_n.append(state[0])
            l_n.append(state[1])
            acc_n.append(state[2])
            tile_max_n.append(None if scored is None else scored[1])
        return m_n, l_n, acc_n, tile_max_n

    per_q = tq // tk
    assert per_q == 2
    d0, d1 = qi * per_q, qi * per_q + 1
    kd1 = k_ref[pl.ds(pl.multiple_of(d1 * tk, tk), tk), :]
    tile_max = []
    for cols in chunks:
        scored = scores(cols, kd1, tk)
        if scored is not None:
            buf1[:, cols] = scored[0]
        tile_max.append(None if scored is None else scored[1])
    n_c = len(chunks)
    carry = ([jnp.full((1, cc), NEG_BIG, F32)] * n_c, [jnp.zeros((1, cc), F32)] * n_c,
             [jnp.zeros((LANES, cc), F32)] * n_c, tile_max)
    carry = step(d0, d1, 0, buf1, buf0, carry)

    def pair(j, carry):
        kb = d0 - 1 - 2 * j
        carry = step(kb, kb + 1, None, buf0, buf1, carry)
        return step(kb - 1, kb, None, buf1, buf0, carry)

    m, l, acc, tile_max = lax.fori_loop(0, qi, pair, carry)
    v_last = v_ref[pl.ds(0, tk), :]
    outs = [accumulate(buf0[:, cols], tile_max[c], m[c], l[c], acc[c], v_last)
            for c, cols in enumerate(chunks)]
    o = jnp.concatenate([oc[2] / oc[1] for oc in outs], axis=1)

    lp = lam_ref[...]
    e1 = jnp.exp(jnp.sum(lp[0:1, :] * lp[1:2, :], axis=1, keepdims=True))
    e2 = jnp.exp(jnp.sum(lp[2:3, :] * lp[3:4, :], axis=1, keepdims=True))
    lam = e1 - e2 + lam_init
    o = o[:, :tq] - lam * o[:, tq:]
    ms = jnp.mean(o * o, axis=0, keepdims=True)
    o = o * lax.rsqrt(ms + NORM_EPS) * subln_ref[...] * (1.0 - lam_init)
    return o.T.astype(BF16)


def _diff_attn(proj, lam_p, subln, lam_init):
    b, s, _ = proj.shape
    kcol0 = DIFF_WIDTH // LANES
    vcol0 = 2 * DIFF_WIDTH // LANES
    return pl.pallas_call(
        functools.partial(_diff_attn_kernel, lam_init=lam_init),
        grid=(b, DIFF_HEADS),
        in_specs=[
            pl.BlockSpec((None, s, LANES), lambda bi, h: (bi, 0, h)),
            pl.BlockSpec((None, s, LANES), lambda bi, h: (bi, 0, kcol0 + h)),
            pl.BlockSpec((None, s, LANES), lambda bi, h: (bi, 0, vcol0 + h)),
            pl.BlockSpec((4, HEAD_DIM), lambda bi, h: (0, 0)),
            pl.BlockSpec((LANES, 1), lambda bi, h: (0, 0)),
        ],
        out_specs=pl.BlockSpec((None, s, LANES), lambda bi, h: (bi, 0, h)),
        out_shape=jax.ShapeDtypeStruct((b, s, DIFF_WIDTH), BF16),
        scratch_shapes=[pltpu.VMEM((DIFF_TK, 2 * DIFF_TQ), F32)] * 2,
        compiler_params=pltpu.CompilerParams(
            dimension_semantics=("arbitrary", "arbitrary"), vmem_limit_bytes=VMEM_LIMIT),
        name="diff_attn",
    )(proj, proj, proj, lam_p, subln)


def _sb_attn_kernel(q_ref, k_ref, v_ref, o_ref, buf0, buf1):
    tq = SB_TQ

    def q_tile(qi, carry):
        rows = pl.ds(pl.multiple_of(qi * tq, tq), tq)
        o_ref[rows, :] = _sb_q_tile(qi, q_ref[rows, :], k_ref, v_ref, buf0, buf1)
        return carry

    lax.fori_loop(0, q_ref.shape[0] // tq, q_tile, 0)


def _sb_q_tile(qi, q, k_ref, v_ref, buf0, buf1):
    tq, tk, cc = SB_TQ, SB_TK, SB_CC
    lane = lax.broadcasted_iota(jnp.int32, q.shape, 1)
    zero = jnp.zeros_like(q)
    per_q = tq // tk
    qs = jnp.concatenate([jnp.where(lane < HEAD_DIM, q, zero),
                          jnp.where(lane >= HEAD_DIM, q, zero)], axis=0)
    ss = lax.broadcasted_iota(jnp.int32, (tk, 2 * tk), 0)
    jj = lax.broadcasted_iota(jnp.int32, (tk, 2 * tk), 1) % tk
    neg_suffix = jnp.where(jj >= ss, -1.0, 0.0).astype(BF16)
    sign_bit = jnp.uint32(0x80000000)
    chunks = [slice(c * cc, (c + 1) * cc) for c in range(2 * tq // cc)]
    key = lax.broadcasted_iota(jnp.int32, (tk, cc), 0)
    qry = lax.broadcasted_iota(jnp.int32, (tk, cc), 1)

    def scores(cols, k, diag):
        q0 = cols.start % tq
        if diag is not None and q0 + cc <= diag + 1:
            return None, jnp.zeros((1, cc), F32)
        if diag is not None and q0 >= diag + tk:
            diag = None
        z = _nt_dot(k, qs[cols])
        neg_abs = lax.bitcast_convert_type(
            lax.bitcast_convert_type(z, jnp.uint32) | sign_bit, F32)
        sp = jnp.maximum(z, 0.0) + jnp.log2(1.0 + jnp.exp2(neg_abs))
        if diag is not None:
            mask = (key + diag) < (qry + q0)
            sp = jnp.where(mask, sp, 0.0)
        hi = sp.astype(BF16)
        lo = (sp - hi.astype(F32)).astype(BF16)
        arg = z + jnp.dot(neg_suffix, jnp.concatenate([hi, lo], axis=0),
                          preferred_element_type=F32)
        if diag is not None:
            arg = jnp.where(mask, arg, NEG_BIG)
        return arg, jnp.sum(sp, axis=0, keepdims=True)

    def accumulate(arg, rem, acc, v):
        a = jnp.exp2(arg + rem)
        return acc + _tn_dot(v, a.astype(BF16))

    def step(kb_next, kb_pending, diag, buf_in, buf_out, carry, pending_empty=(),
             exists=None):
        rem, acc, key_sum = carry
        k = k_ref[pl.ds(pl.multiple_of(kb_next * tk, tk), tk), :]
        v = v_ref[pl.ds(pl.multiple_of(kb_pending * tk, tk), tk), :]
        if exists is not None:
            exists_row = jnp.full((1, cc), exists, jnp.int32) > 0
        rem_n, acc_n, key_sum_n = [], [], []
        for c, cols in enumerate(chunks):
            arg_next, ks = scores(cols, k, diag)
            if exists is not None:
                arg_next = jnp.where(exists_row, arg_next, NEG_BIG)
                ks = jnp.where(exists_row, ks, 0.0)
            buf_out[:, cols] = arg_next
            acc_n.append(acc[c] if c in pending_empty
                         else accumulate(buf_in[:, cols], rem[c], acc[c], v))
            rem_n.append(rem[c] - key_sum[c])
            key_sum_n.append(ks)
        return rem_n, acc_n, key_sum_n

    def finish(buf, kb_pending, carry):
        rem, acc, _ = carry
        v = v_ref[pl.ds(pl.multiple_of(kb_pending * tk, tk), tk), :]
        return jnp.concatenate([accumulate(buf[:, cols], rem[c], acc[c], v)
                                for c, cols in enumerate(chunks)], axis=1)

    def next_live(carry):
        rem, _, key_sum = carry
        top = functools.reduce(jnp.maximum, [r - ks for r, ks in zip(rem, key_sum)])
        return (jnp.max(top) > SB_DEAD_LOG2).astype(jnp.int32)

    assert per_q == 2
    d1, d0 = qi * per_q + 1, qi * per_q
    f1 = jnp.maximum(d0 - 1, 0)
    k1 = k_ref[pl.ds(pl.multiple_of(d1 * tk, tk), tk), :]
    key_sum, empty = [], []
    for c, cols in enumerate(chunks):
        arg, ks = scores(cols, k1, tk)
        if arg is None:
            empty.append(c)
        else:
            buf0[:, cols] = arg
        key_sum.append(ks)
    carry = ([jnp.zeros((1, cc), F32)] * len(chunks),
             [jnp.zeros((LANES, cc), F32)] * len(chunks), key_sum)
    carry = step(d0, d1, 0, buf0, buf1, carry, pending_empty=empty)
    carry = step(f1, d0, None, buf1, buf0, carry, exists=(qi > 0).astype(jnp.int32))

    def pair(state):
        j, _, carry = state
        kb = f1 - 1 - 2 * j
        carry = step(kb, kb + 1, None, buf0, buf1, carry)
        carry = step(kb - 1, kb, None, buf1, buf0, carry)
        return j + 1, next_live(carry), carry

    n_pairs, go, carry = lax.while_loop(
        lambda st: (st[0] < qi - 1) & (st[1] > 0), pair,
        (jnp.int32(0), next_live(carry), carry))
    pending = f1 - 2 * n_pairs

    def leftover():
        return finish(buf1, 0, step(0, 1, None, buf0, buf1, carry))

    acc = lax.cond((pending == 1) & (go > 0), leftover, lambda: finish(buf0, pending, carry))
    feat = lax.broadcasted_iota(jnp.int32, (LANES, tq), 0)
    return jnp.where(feat < HEAD_DIM, acc[:, :tq], acc[:, tq:]).T.astype(BF16)


def _sb_attn(proj):
    b, s, _ = proj.shape
    qcol0 = 3 * DIFF_WIDTH // LANES
    kcol0 = qcol0 + SB_PAIRS
    vcol0 = kcol0 + SB_PAIRS
    return pl.pallas_call(
        _sb_attn_kernel,
        grid=(b, SB_PAIRS),
        in_specs=[
            pl.BlockSpec((None, s, LANES), lambda bi, p: (bi, 0, qcol0 + p)),
            pl.BlockSpec((None, s, LANES), lambda bi, p: (bi, 0, kcol0 + p)),
            pl.BlockSpec((None, s, LANES), lambda bi, p: (bi, 0, vcol0 + p)),
        ],
        out_specs=pl.BlockSpec((None, s, LANES), lambda bi, p: (bi, 0, p)),
        out_shape=jax.ShapeDtypeStruct((b, s, SB_WIDTH), BF16),
        scratch_shapes=[pltpu.VMEM((SB_TK, 2 * SB_TQ), F32)] * 2,
        compiler_params=pltpu.CompilerParams(
            dimension_semantics=("arbitrary", "arbitrary"), vmem_limit_bytes=VMEM_LIMIT),
        name="sb_attn",
    )(proj, proj, proj)


def _merge_ffn_kernel(x_ref, od_ref, os_ref, gd_ref, gs_ref, wod_ref, wos_ref, wo_ref,
                      g_ref, wi_ref, wd_ref, gf_ref, o_ref, *, final_norm):
    pd = jnp.dot(od_ref[...], wod_ref[...], preferred_element_type=F32)
    ps = jnp.dot(os_ref[...], wos_ref[...], preferred_element_type=F32)
    merged = gd_ref[...].astype(F32) * pd + gs_ref[...].astype(F32) * ps
    x = x_ref[...] + jnp.dot(merged.astype(BF16), wo_ref[...], preferred_element_type=F32)

    ms = jnp.mean(x * x, axis=-1, keepdims=True)
    h = (x * lax.rsqrt(ms + NORM_EPS) * g_ref[...]).astype(BF16)
    y = x
    for c0 in range(0, FFN_HIDDEN, FFN_CH):
        c1 = min(c0 + FFN_CH, FFN_HIDDEN)
        gate = jnp.dot(h, wi_ref[:, c0:c1], preferred_element_type=F32)
        up = jnp.dot(h, wi_ref[:, FFN_HIDDEN + c0:FFN_HIDDEN + c1],
                     preferred_element_type=F32)
        act = (gate * jax.nn.sigmoid(gate) * up).astype(BF16)
        y = y + jnp.dot(act, wd_ref[c0:c1, :], preferred_element_type=F32)
    if final_norm:
        ms = jnp.mean(y * y, axis=-1, keepdims=True)
        y = y * lax.rsqrt(ms + NORM_EPS) * gf_ref[...]
    o_ref[...] = y


def _merge_ffn(x2d, o_diff, o_sb, proj2d, w_o_diff, w_o_sb, w_out, g, w_in, w_down, g_final,
               final_norm):
    t_rows = x2d.shape[0]
    tm = FFN_TM
    gd_blk = GATE_COL0 // D_MODEL
    once = pl.Buffered(1)
    rows = lambda width, blk=0: pl.BlockSpec((tm, width), lambda i: (i, blk))
    whole = lambda shape: pl.BlockSpec(shape, lambda i: (0, 0), pipeline_mode=once)
    return pl.pallas_call(
        functools.partial(_merge_ffn_kernel, final_norm=final_norm),
        grid=(t_rows // tm,),
        in_specs=[
            rows(D_MODEL), rows(DIFF_WIDTH), rows(SB_WIDTH),
            rows(D_MODEL, gd_blk), rows(D_MODEL, gd_blk + 1),
            whole((DIFF_WIDTH, D_MODEL)), whole((SB_WIDTH, D_MODEL)),
            whole((D_MODEL, D_MODEL)), whole((1, D_MODEL)),
            whole((D_MODEL, 2 * FFN_HIDDEN)), whole((FFN_HIDDEN, D_MODEL)),
            whole((1, D_MODEL)),
        ],
        out_specs=rows(D_MODEL),
        out_shape=jax.ShapeDtypeStruct((t_rows, D_MODEL), F32),
        compiler_params=pltpu.CompilerParams(
            dimension_semantics=("arbitrary",), vmem_limit_bytes=VMEM_LIMIT),
        name="merge_ffn",
    )(x2d, o_diff, o_sb, proj2d, proj2d, w_o_diff, w_o_sb, w_out, g, w_in, w_down, g_final)


def _rope_tables(seq):
    half = HEAD_DIM // 2
    pos = jnp.arange(seq, dtype=F32)
    inv = ROPE_THETA ** (-jnp.arange(0, HEAD_DIM, 2, dtype=F32) / HEAD_DIM)
    ang = pos[:, None] * inv[None, :]
    cos, sin = jnp.cos(ang), jnp.sin(ang)
    cos_t = jnp.tile(cos, (1, LANES // half))
    sin_t = jnp.tile(jnp.concatenate([-sin, sin], axis=1), (1, LANES // HEAD_DIM))
    return cos_t, sin_t


def kernel(x, norm_attn, w_in, b_gate, diff_lambda, diff_subln, w_o_diff, w_o_sb, w_out,
           norm_ffn, w_ffn_in, w_ffn_out, norm_final):
    b, s, d = x.shape
    cos_t, sin_t = _rope_tables(s)
    x2d = x.reshape(b * s, d)
    g_final = norm_final.reshape(1, d)
    for layer in range(DEPTH):
        lam_init = 0.8 - 0.6 * math.exp(-0.3 * layer)
        proj2d = _in_proj(x2d, norm_attn[layer].reshape(1, d), w_in[layer].astype(BF16),
                          b_gate[layer].reshape(1, 2 * d), cos_t, sin_t, s)
        proj = proj2d.reshape(b, s, IN_COLS)
        o_diff = _diff_attn(proj, diff_lambda[layer], diff_subln[layer].reshape(LANES, 1),
                            lam_init)
        o_sb = _sb_attn(proj)
        x2d = _merge_ffn(
            x2d, o_diff.reshape(b * s, DIFF_WIDTH), o_sb.reshape(b * s, SB_WIDTH), proj2d,
            w_o_diff[layer].astype(BF16), w_o_sb[layer].astype(BF16),
            w_out[layer].astype(BF16), norm_ffn[layer].reshape(1, d),
            w_ffn_in[layer].astype(BF16), w_ffn_out[layer].astype(BF16), g_final,
            layer == DEPTH - 1)
    return x2d.reshape(b, s, d)
```

```python
import functools
import math

import jax
import jax.numpy as jnp
from jax import lax
from jax.experimental import pallas as pl
from jax.experimental.pallas import tpu as pltpu

D_MODEL = 1024
DEPTH = 2
HEAD_DIM = 64
LANES = 128
DIFF_HEADS = 4
DIFF_WIDTH = DIFF_HEADS * 2 * HEAD_DIM
SB_HEADS = 8
SB_WIDTH = SB_HEADS * HEAD_DIM
SB_PAIRS = SB_WIDTH // LANES
IN_COLS = 3 * DIFF_WIDTH + 3 * SB_WIDTH + 2 * D_MODEL
GATE_COL0 = 3 * DIFF_WIDTH + 3 * SB_WIDTH
FFN_HIDDEN = 2816
ROPE_THETA = 10000.0
NORM_EPS = 1e-6
NEG_BIG = -1e30
LOG2E = math.log2(math.e)
SB_DEAD_LOG2 = -160.0

VMEM_LIMIT = 48 * 1024 * 1024
ATTN_VMEM_LIMIT = 56 * 1024 * 1024

IN_TM, IN_CW = 512, 512
DIFF_TQ, DIFF_TK, DIFF_CC = 1024, 512, 256
SB_TQ, SB_TK, SB_CC = 512, 256, 256
FFN_TM, FFN_CH = 512, 512

BF16 = jnp.bfloat16
F32 = jnp.float32


def _nt_dot(a, b):
    return lax.dot_general(a, b, (((1,), (1,)), ((), ())), preferred_element_type=F32)


def _in_proj_kernel(x_ref, g_ref, w_ref, bg_ref, cos_ref, sin_ref, o_ref):
    x = x_ref[...]
    ms = jnp.mean(x * x, axis=-1, keepdims=True)
    h = (x * lax.rsqrt(ms + NORM_EPS) * g_ref[...]).astype(BF16)

    scale = HEAD_DIM ** -0.5 * LOG2E
    cw = IN_CW
    assert DIFF_WIDTH % cw == 0 and SB_WIDTH % cw == 0
    cos = cos_ref[...]
    sin = sin_ref[...]
    lane = lax.broadcasted_iota(jnp.int32, cos.shape, 1)
    first_half = (lane % HEAD_DIM) < (HEAD_DIM // 2)

    for c0 in range(0, IN_COLS, cw):
        acc = jnp.dot(h, w_ref[:, c0:c0 + cw], preferred_element_type=F32)
        is_q = c0 < DIFF_WIDTH or 3 * DIFF_WIDTH <= c0 < 3 * DIFF_WIDTH + SB_WIDTH
        if c0 < 2 * DIFF_WIDTH:
            for b0 in range(0, cw, LANES):
                t = acc[:, b0:b0 + LANES]
                partner = jnp.where(first_half,
                                    pltpu.roll(t, LANES - HEAD_DIM // 2, 1),
                                    pltpu.roll(t, HEAD_DIM // 2, 1))
                r = t * cos + partner * sin
                if is_q:
                    r = r * scale
                o_ref[:, c0 + b0:c0 + b0 + LANES] = r.astype(BF16)
        elif c0 < GATE_COL0:
            if is_q:
                acc = acc * scale
            o_ref[:, c0:c0 + cw] = acc.astype(BF16)
        else:
            g0 = c0 - GATE_COL0
            o_ref[:, c0:c0 + cw] = jax.nn.sigmoid(acc + bg_ref[:, g0:g0 + cw]).astype(BF16)


def _in_proj(x2d, g, w, bg, cos_t, sin_t, seq):
    t_rows = x2d.shape[0]
    pos_tiles = seq // IN_TM
    return pl.pallas_call(
        _in_proj_kernel,
        grid=(t_rows // IN_TM,),
        in_specs=[
            pl.BlockSpec((IN_TM, D_MODEL), lambda i: (i, 0)),
            pl.BlockSpec((1, D_MODEL), lambda i: (0, 0)),
            pl.BlockSpec((D_MODEL, IN_COLS), lambda i: (0, 0)),
            pl.BlockSpec((1, 2 * D_MODEL), lambda i: (0, 0)),
            pl.BlockSpec((IN_TM, LANES), lambda i: (i % pos_tiles, 0)),
            pl.BlockSpec((IN_TM, LANES), lambda i: (i % pos_tiles, 0)),
        ],
        out_specs=pl.BlockSpec((IN_TM, IN_COLS), lambda i: (i, 0)),
        out_shape=jax.ShapeDtypeStruct((t_rows, IN_COLS), BF16),
        compiler_params=pltpu.CompilerParams(
            dimension_semantics=("arbitrary",), vmem_limit_bytes=VMEM_LIMIT),
        name="in_proj",
    )(x2d, g, w, bg, cos_t, sin_t)


def _tn_dot(a, b):
    return lax.dot_general(a, b, (((0,), (0,)), ((), ())), preferred_element_type=F32)


def _diff_tile(qi, q, k_ref, v_ref, lam_ref, subln_ref, buf0, buf1, lam_init):
    tq, tk, cc = DIFF_TQ, DIFF_TK, DIFF_CC
    lane = lax.broadcasted_iota(jnp.int32, q.shape, 1)
    zero = jnp.zeros_like(q)
    qs = jnp.concatenate([jnp.where(lane < HEAD_DIM, q, zero),
                          jnp.where(lane >= HEAD_DIM, q, zero)], axis=0)
    chunks = [slice(c * cc, (c + 1) * cc) for c in range(2 * tq // cc)]
    key = lax.broadcasted_iota(jnp.int32, (tk, cc), 0)
    qry = lax.broadcasted_iota(jnp.int32, (tk, cc), 1)

    def scores(cols, k, diag):
        q0 = cols.start % tq
        if diag is not None and q0 + cc <= diag:
            return None
        s = _nt_dot(k, qs[cols])
        if diag is not None and q0 < diag + tk - 1:
            s = jnp.where((key + diag) <= (qry + q0), s, NEG_BIG)
        return s, jnp.max(s, axis=0, keepdims=True)

    def accumulate(s, tile_max, m, l, acc, v):
        m_new = jnp.maximum(m, tile_max)
        alpha = jnp.exp2(m - m_new)
        p = jnp.exp2(s - m_new)
        l = alpha * l + jnp.sum(p, axis=0, keepdims=True)
        acc = alpha * acc + _tn_dot(v, p.astype(BF16))
        return m_new, l, acc

    def step(kb_next, kb_pending, diag, buf_in, buf_out, carry):
        m, l, acc, tile_max = carry
        k = k_ref[pl.ds(pl.multiple_of(kb_next * tk, tk), tk), :]
        v = v_ref[pl.ds(pl.multiple_of(kb_pending * tk, tk), tk), :]
        m_n, l_n, acc_n, tile_max_n = [], [], [], []
        for c, cols in enumerate(chunks):
            scored = scores(cols, k, diag)
            if scored is not None:
                buf_out[:, cols] = scored[0]
            state = (m[c], l[c], acc[c])
            if tile_max[c] is not None:
                state = accumulate(buf_in[:, cols], tile_max[c], *state, v)
            m_n.append(state[0])
            l_n.append(state[1])
            acc_n.append(state[2])
            tile_max_n.append(None if scored is None else scored[1])
        return m_n, l_n, acc_n, tile_max_n

    per_q = tq // tk
    assert per_q == 2
    d0, d1 = qi * per_q, qi * per_q + 1

    def begin():
        kd1 = k_ref[pl.ds(pl.multiple_of(d1 * tk, tk), tk), :]
        tile_max = []
        for cols in chunks:
            scored = scores(cols, kd1, tk)
            if scored is not None:
                buf1[:, cols] = scored[0]
            tile_max.append(None if scored is None else scored[1])
        n_c = len(chunks)
        carry = ([jnp.full((1, cc), NEG_BIG, F32)] * n_c, [jnp.zeros((1, cc), F32)] * n_c,
                 [jnp.zeros((LANES, cc), F32)] * n_c, tile_max)
        return step(d0, d1, 0, buf1, buf0, carry)

    def pair(j, carry):
        kb = d0 - 1 - 2 * j
        carry = step(kb, kb + 1, None, buf0, buf1, carry)
        return step(kb - 1, kb, None, buf1, buf0, carry)

    def walk(carry):
        return lax.fori_loop(0, qi, pair, carry)

    def finish(carry):
        m, l, acc, tile_max = carry
        v_last = v_ref[pl.ds(0, tk), :]
        outs = [accumulate(buf0[:, cols], tile_max[c], m[c], l[c], acc[c], v_last)
                for c, cols in enumerate(chunks)]
        o = jnp.concatenate([oc[2] * (1.0 / oc[1]) for oc in outs], axis=1)

        lp = lam_ref[...]
        e1 = jnp.exp(jnp.sum(lp[0:1, :] * lp[1:2, :], axis=1, keepdims=True))
        e2 = jnp.exp(jnp.sum(lp[2:3, :] * lp[3:4, :], axis=1, keepdims=True))
        lam = e1 - e2 + lam_init
        o = o[:, :tq] - lam * o[:, tq:]
        ms = jnp.mean(o * o, axis=0, keepdims=True)
        o = o * lax.rsqrt(ms + NORM_EPS) * subln_ref[...] * (1.0 - lam_init)
        return o.T.astype(BF16)

    return begin, walk, finish


def _sb_tile(qi, q, k_ref, v_ref, buf0, buf1):
    tq, tk, cc = SB_TQ, SB_TK, SB_CC
    lane = lax.broadcasted_iota(jnp.int32, q.shape, 1)
    zero = jnp.zeros_like(q)
    per_q = tq // tk
    qs = jnp.concatenate([jnp.where(lane < HEAD_DIM, q, zero),
                          jnp.where(lane >= HEAD_DIM, q, zero)], axis=0)
    ss = lax.broadcasted_iota(jnp.int32, (tk, 2 * tk), 0)
    jj = lax.broadcasted_iota(jnp.int32, (tk, 2 * tk), 1) % tk
    neg_suffix = jnp.where(jj >= ss, -1.0, 0.0).astype(BF16)
    sign_bit = jnp.uint32(0x80000000)
    chunks = [slice(c * cc, (c + 1) * cc) for c in range(2 * tq // cc)]
    key = lax.broadcasted_iota(jnp.int32, (tk, cc), 0)
    qry = lax.broadcasted_iota(jnp.int32, (tk, cc), 1)

    def scores(cols, k, diag):
        q0 = cols.start % tq
        if diag is not None and q0 + cc <= diag + 1:
            return None, jnp.zeros((1, cc), F32)
        if diag is not None and q0 >= diag + tk:
            diag = None
        z = _nt_dot(k, qs[cols])
        neg_abs = lax.bitcast_convert_type(
            lax.bitcast_convert_type(z, jnp.uint32) | sign_bit, F32)
        sp = jnp.maximum(z, 0.0) + jnp.log2(1.0 + jnp.exp2(neg_abs))
        if diag is not None:
            mask = (key + diag) < (qry + q0)
            sp = jnp.where(mask, sp, 0.0)
        hi = sp.astype(BF16)
        lo = (sp - hi.astype(F32)).astype(BF16)
        arg = z + jnp.dot(neg_suffix, jnp.concatenate([hi, lo], axis=0),
                          preferred_element_type=F32)
        if diag is not None:
            arg = jnp.where(mask, arg, NEG_BIG)
        return arg, jnp.sum(sp, axis=0, keepdims=True)

    def accumulate(arg, rem, acc, v):
        a = jnp.exp2(arg + rem)
        return acc + _tn_dot(v, a.astype(BF16))

    def step(kb_next, kb_pending, diag, buf_in, buf_out, carry, pending_empty=(),
             exists=None):
        rem, acc, key_sum = carry
        k = k_ref[pl.ds(pl.multiple_of(kb_next * tk, tk), tk), :]
        v = v_ref[pl.ds(pl.multiple_of(kb_pending * tk, tk), tk), :]
        if exists is not None:
            exists_row = jnp.full((1, cc), exists, jnp.int32) > 0
        rem_n, acc_n, key_sum_n = [], [], []
        for c, cols in enumerate(chunks):
            arg_next, ks = scores(cols, k, diag)
            if exists is not None:
                arg_next = jnp.where(exists_row, arg_next, NEG_BIG)
                ks = jnp.where(exists_row, ks, 0.0)
            buf_out[:, cols] = arg_next
            acc_n.append(acc[c] if c in pending_empty
                         else accumulate(buf_in[:, cols], rem[c], acc[c], v))
            rem_n.append(rem[c] - key_sum[c])
            key_sum_n.append(ks)
        return rem_n, acc_n, key_sum_n

    def next_live(carry):
        rem, _, key_sum = carry
        top = functools.reduce(jnp.maximum, [r - ks for r, ks in zip(rem, key_sum)])
        return (jnp.max(top) > SB_DEAD_LOG2).astype(jnp.int32)

    assert per_q == 2
    d1, d0 = qi * per_q + 1, qi * per_q
    f1 = jnp.maximum(d0 - 1, 0)

    def begin():
        k1 = k_ref[pl.ds(pl.multiple_of(d1 * tk, tk), tk), :]
        key_sum, empty = [], []
        for c, cols in enumerate(chunks):
            arg, ks = scores(cols, k1, tk)
            if arg is None:
                empty.append(c)
            else:
                buf0[:, cols] = arg
            key_sum.append(ks)
        carry = ([jnp.zeros((1, cc), F32)] * len(chunks),
                 [jnp.zeros((LANES, cc), F32)] * len(chunks), key_sum)
        carry = step(d0, d1, 0, buf0, buf1, carry, pending_empty=empty)
        return step(f1, d0, None, buf1, buf0, carry, exists=(qi > 0).astype(jnp.int32))

    def pair(state):
        j, _, carry = state
        kb = f1 - 1 - 2 * j
        carry = step(kb, kb + 1, None, buf0, buf1, carry)
        carry = step(jnp.maximum(kb - 1, 0), kb, None, buf1, buf0, carry,
                     exists=(kb > 0).astype(jnp.int32))
        return j + 1, next_live(carry), carry

    def walk(carry):
        n_pairs, _, carry = lax.while_loop(
            lambda st: (st[0] < qi) & (st[1] > 0), pair,
            (jnp.int32(0), next_live(carry), carry))
        return n_pairs, carry

    def finish(state):
        n_pairs, (rem, acc, _) = state
        pending = jnp.maximum(f1 - 2 * n_pairs, 0)
        v = v_ref[pl.ds(pl.multiple_of(pending * tk, tk), tk), :]
        acc = jnp.concatenate([accumulate(buf0[:, cols], rem[c], acc[c], v)
                               for c, cols in enumerate(chunks)], axis=1)
        feat = lax.broadcasted_iota(jnp.int32, (LANES, tq), 0)
        return jnp.where(feat < HEAD_DIM, acc[:, :tq], acc[:, tq:]).T.astype(BF16)

    return begin, walk, finish


def _attn_kernel(dq_ref, dk_ref, dv_ref, lam_ref, subln_ref, sq_ref, sk_ref, sv_ref,
                 od_ref, os_ref, dbuf0, dbuf1, sbuf00, sbuf01, sbuf10, sbuf11, *, lam_init):
    ratio = DIFF_TQ // SB_TQ
    sbufs = ((sbuf00, sbuf01), (sbuf10, sbuf11))
    assert ratio == len(sbufs)

    def q_group(g, carry):
        d_rows = pl.ds(pl.multiple_of(g * DIFF_TQ, DIFF_TQ), DIFF_TQ)
        s_rows = [pl.ds(pl.multiple_of((ratio * g + i) * SB_TQ, SB_TQ), SB_TQ)
                  for i in range(ratio)]
        tiles = [_diff_tile(g, dq_ref[d_rows, :], dk_ref, dv_ref, lam_ref, subln_ref,
                            dbuf0, dbuf1, lam_init)]
        tiles += [_sb_tile(ratio * g + i, sq_ref[s_rows[i], :], sk_ref, sv_ref, *sbufs[i])
                  for i in range(ratio)]
        states = [begin() for begin, _, _ in tiles]
        states = [walk(st) for (_, walk, _), st in zip(tiles, states)]
        outs = [finish(st) for (_, _, finish), st in zip(tiles, states)]
        od_ref[d_rows, :] = outs[0]
        for i in range(ratio):
            os_ref[s_rows[i], :] = outs[1 + i]
        return carry

    lax.fori_loop(0, dq_ref.shape[0] // DIFF_TQ, q_group, 0)


def _attention(proj, lam_p, subln, lam_init):
    b, s, _ = proj.shape
    assert DIFF_HEADS == SB_PAIRS
    col = lambda first: pl.BlockSpec((None, s, LANES), lambda bi, h: (bi, 0, first + h))
    dq, dk, dv = (i * DIFF_WIDTH // LANES for i in range(3))
    sq, sk, sv = (3 * DIFF_WIDTH // LANES + i * SB_PAIRS for i in range(3))
    return pl.pallas_call(
        functools.partial(_attn_kernel, lam_init=lam_init),
        grid=(b, DIFF_HEADS),
        in_specs=[
            col(dq), col(dk), col(dv),
            pl.BlockSpec((4, HEAD_DIM), lambda bi, h: (0, 0)),
            pl.BlockSpec((LANES, 1), lambda bi, h: (0, 0)),
            col(sq), col(sk), col(sv),
        ],
        out_specs=[col(0), col(0)],
        out_shape=[jax.ShapeDtypeStruct((b, s, DIFF_WIDTH), BF16),
                   jax.ShapeDtypeStruct((b, s, SB_WIDTH), BF16)],
        scratch_shapes=([pltpu.VMEM((DIFF_TK, 2 * DIFF_TQ), F32)] * 2
                        + [pltpu.VMEM((SB_TK, 2 * SB_TQ), F32)] * 4),
        compiler_params=pltpu.CompilerParams(
            dimension_semantics=("arbitrary", "arbitrary"), vmem_limit_bytes=ATTN_VMEM_LIMIT),
        name="attention",
    )(proj, proj, proj, lam_p, subln, proj, proj, proj)


def _merge_ffn_kernel(x_ref, od_ref, os_ref, gd_ref, gs_ref, wod_ref, wos_ref, wo_ref,
                      g_ref, wi_ref, wd_ref, gf_ref, o_ref, *, final_norm):
    pd = jnp.dot(od_ref[...], wod_ref[...], preferred_element_type=F32)
    ps = jnp.dot(os_ref[...], wos_ref[...], preferred_element_type=F32)
    merged = gd_ref[...].astype(F32) * pd + gs_ref[...].astype(F32) * ps
    x = x_ref[...] + jnp.dot(merged.astype(BF16), wo_ref[...], preferred_element_type=F32)

    ms = jnp.mean(x * x, axis=-1, keepdims=True)
    h = (x * lax.rsqrt(ms + NORM_EPS) * g_ref[...]).astype(BF16)
    y = x
    for c0 in range(0, FFN_HIDDEN, FFN_CH):
        c1 = min(c0 + FFN_CH, FFN_HIDDEN)
        gate = jnp.dot(h, wi_ref[:, c0:c1], preferred_element_type=F32)
        up = jnp.dot(h, wi_ref[:, FFN_HIDDEN + c0:FFN_HIDDEN + c1],
                     preferred_element_type=F32)
        act = (gate * jax.nn.sigmoid(gate) * up).astype(BF16)
        y = y + jnp.dot(act, wd_ref[c0:c1, :], preferred_element_type=F32)
    if final_norm:
        ms = jnp.mean(y * y, axis=-1, keepdims=True)
        y = y * lax.rsqrt(ms + NORM_EPS) * gf_ref[...]
    o_ref[...] = y


def _merge_ffn(x2d, o_diff, o_sb, proj2d, w_o_diff, w_o_sb, w_out, g, w_in, w_down, g_final,
               final_norm):
    t_rows = x2d.shape[0]
    tm = FFN_TM
    gd_blk = GATE_COL0 // D_MODEL
    once = pl.Buffered(1)
    rows = lambda width, blk=0: pl.BlockSpec((tm, width), lambda i: (i, blk))
    whole = lambda shape: pl.BlockSpec(shape, lambda i: (0, 0), pipeline_mode=once)
    return pl.pallas_call(
        functools.partial(_merge_ffn_kernel, final_norm=final_norm),
        grid=(t_rows // tm,),
        in_specs=[
            rows(D_MODEL), rows(DIFF_WIDTH), rows(SB_WIDTH),
            rows(D_MODEL, gd_blk), rows(D_MODEL, gd_blk + 1),
            whole((DIFF_WIDTH, D_MODEL)), whole((SB_WIDTH, D_MODEL)),
            whole((D_MODEL, D_MODEL)), whole((1, D_MODEL)),
            whole((D_MODEL, 2 * FFN_HIDDEN)), whole((FFN_HIDDEN, D_MODEL)),
            whole((1, D_MODEL)),
        ],
        out_specs=rows(D_MODEL),
        out_shape=jax.ShapeDtypeStruct((t_rows, D_MODEL), F32),
        compiler_params=pltpu.CompilerParams(
            dimension_semantics=("arbitrary",), vmem_limit_bytes=VMEM_LIMIT),
        name="merge_ffn",
    )(x2d, o_diff, o_sb, proj2d, proj2d, w_o_diff, w_o_sb, w_out, g, w_in, w_down, g_final)


def _rope_tables(seq):
    half = HEAD_DIM // 2
    pos = jnp.arange(seq, dtype=F32)
    inv = ROPE_THETA ** (-jnp.arange(0, HEAD_DIM, 2, dtype=F32) / HEAD_DIM)
    ang = pos[:, None] * inv[None, :]
    cos, sin = jnp.cos(ang), jnp.sin(ang)
    cos_t = jnp.tile(cos, (1, LANES // half))
    sin_t = jnp.tile(jnp.concatenate([-sin, sin], axis=1), (1, LANES // HEAD_DIM))
    return cos_t, sin_t


def kernel(x, norm_attn, w_in, b_gate, diff_lambda, diff_subln, w_o_diff, w_o_sb, w_out,
           norm_ffn, w_ffn_in, w_ffn_out, norm_final):
    b, s, d = x.shape
    cos_t, sin_t = _rope_tables(s)
    x2d = x.reshape(b * s, d)
    g_final = norm_final.reshape(1, d)
    for layer in range(DEPTH):
        lam_init = 0.8 - 0.6 * math.exp(-0.3 * layer)
        proj2d = _in_proj(x2d, norm_attn[layer].reshape(1, d), w_in[layer].astype(BF16),
                          b_gate[layer].reshape(1, 2 * d), cos_t, sin_t, s)
        proj = proj2d.reshape(b, s, IN_COLS)
        o_diff, o_sb = _attention(proj, diff_lambda[layer],
                                  diff_subln[layer].reshape(LANES, 1), lam_init)
        x2d = _merge_ffn(
            x2d, o_diff.reshape(b * s, DIFF_WIDTH), o_sb.reshape(b * s, SB_WIDTH), proj2d,
            w_o_diff[layer].astype(BF16), w_o_sb[layer].astype(BF16),
            w_out[layer].astype(BF16), norm_ffn[layer].reshape(1, d),
            w_ffn_in[layer].astype(BF16), w_ffn_out[layer].astype(BF16), g_final,
            layer == DEPTH - 1)
    return x2d.reshape(b, s, d)
```

```python
import functools
import math

import jax
import jax.numpy as jnp
from jax import lax
from jax.experimental import pallas as pl
from jax.experimental.pallas import tpu as pltpu

D_MODEL = 1024
DEPTH = 2
HEAD_DIM = 64
LANES = 128
DIFF_HEADS = 4
DIFF_WIDTH = DIFF_HEADS * 2 * HEAD_DIM
SB_HEADS = 8
SB_WIDTH = SB_HEADS * HEAD_DIM
SB_PAIRS = SB_WIDTH // LANES
IN_COLS = 3 * DIFF_WIDTH + 3 * SB_WIDTH + 2 * D_MODEL
GATE_COL0 = 3 * DIFF_WIDTH + 3 * SB_WIDTH
FFN_HIDDEN = 2816
ROPE_THETA = 10000.0
NORM_EPS = 1e-6
NEG_BIG = -1e30
LOG2E = math.log2(math.e)
SB_DEAD_LOG2 = -160.0

VMEM_LIMIT = 48 * 1024 * 1024
ATTN_VMEM_LIMIT = 56 * 1024 * 1024

IN_TM, IN_CW = 512, 512
DIFF_TQ, DIFF_TK, DIFF_CC = 1024, 512, 256
SB_TQ, SB_TK, SB_CC = 512, 256, 256
FFN_TM, FFN_CH = 512, 512

BF16 = jnp.bfloat16
F32 = jnp.float32


def _nt_dot(a, b):
    return lax.dot_general(a, b, (((1,), (1,)), ((), ())), preferred_element_type=F32)


def _in_proj_kernel(x_ref, g_ref, w_ref, bg_ref, cos_ref, sin_ref, o_ref):
    x = x_ref[...]
    ms = jnp.mean(x * x, axis=-1, keepdims=True)
    h = (x * lax.rsqrt(ms + NORM_EPS) * g_ref[...]).astype(BF16)

    scale = HEAD_DIM ** -0.5 * LOG2E
    cw = IN_CW
    assert DIFF_WIDTH % cw == 0 and SB_WIDTH % cw == 0
    cos = cos_ref[...]
    sin = sin_ref[...]
    lane = lax.broadcasted_iota(jnp.int32, cos.shape, 1)
    first_half = (lane % HEAD_DIM) < (HEAD_DIM // 2)

    for c0 in range(0, IN_COLS, cw):
        acc = jnp.dot(h, w_ref[:, c0:c0 + cw], preferred_element_type=F32)
        is_q = c0 < DIFF_WIDTH or 3 * DIFF_WIDTH <= c0 < 3 * DIFF_WIDTH + SB_WIDTH
        if c0 < 2 * DIFF_WIDTH:
            for b0 in range(0, cw, LANES):
                t = acc[:, b0:b0 + LANES]
                partner = jnp.where(first_half,
                                    pltpu.roll(t, LANES - HEAD_DIM // 2, 1),
                                    pltpu.roll(t, HEAD_DIM // 2, 1))
                r = t * cos + partner * sin
                if is_q:
                    r = r * scale
                o_ref[:, c0 + b0:c0 + b0 + LANES] = r.astype(BF16)
        elif c0 < GATE_COL0:
            if is_q:
                acc = acc * scale
            o_ref[:, c0:c0 + cw] = acc.astype(BF16)
        else:
            g0 = c0 - GATE_COL0
            o_ref[:, c0:c0 + cw] = jax.nn.sigmoid(acc + bg_ref[:, g0:g0 + cw]).astype(BF16)


def _in_proj(x2d, g, w, bg, cos_t, sin_t, seq):
    t_rows = x2d.shape[0]
    pos_tiles = seq // IN_TM
    return pl.pallas_call(
        _in_proj_kernel,
        grid=(t_rows // IN_TM,),
        in_specs=[
            pl.BlockSpec((IN_TM, D_MODEL), lambda i: (i, 0)),
            pl.BlockSpec((1, D_MODEL), lambda i: (0, 0)),
            pl.BlockSpec((D_MODEL, IN_COLS), lambda i: (0, 0)),
            pl.BlockSpec((1, 2 * D_MODEL), lambda i: (0, 0)),
            pl.BlockSpec((IN_TM, LANES), lambda i: (i % pos_tiles, 0)),
            pl.BlockSpec((IN_TM, LANES), lambda i: (i % pos_tiles, 0)),
        ],
        out_specs=pl.BlockSpec((IN_TM, IN_COLS), lambda i: (i, 0)),
        out_shape=jax.ShapeDtypeStruct((t_rows, IN_COLS), BF16),
        compiler_params=pltpu.CompilerParams(
            dimension_semantics=("arbitrary",), vmem_limit_bytes=VMEM_LIMIT),
        name="in_proj",
    )(x2d, g, w, bg, cos_t, sin_t)


def _tn_dot(a, b):
    return lax.dot_general(a, b, (((0,), (0,)), ((), ())), preferred_element_type=F32)


def _diff_tile(qi, q, k_ref, v_ref, lam_ref, subln_ref, buf0, buf1, lam_init):
    tq, tk, cc = DIFF_TQ, DIFF_TK, DIFF_CC
    lane = lax.broadcasted_iota(jnp.int32, q.shape, 1)
    zero = jnp.zeros_like(q)
    qs = jnp.concatenate([jnp.where(lane < HEAD_DIM, q, zero),
                          jnp.where(lane >= HEAD_DIM, q, zero)], axis=0)
    chunks = [slice(c * cc, (c + 1) * cc) for c in range(2 * tq // cc)]
    key = lax.broadcasted_iota(jnp.int32, (tk, cc), 0)
    qry = lax.broadcasted_iota(jnp.int32, (tk, cc), 1)

    def scores(cols, k, diag):
        q0 = cols.start % tq
        if diag is not None and q0 + cc <= diag:
            return None
        s = _nt_dot(k, qs[cols])
        if diag is not None and q0 < diag + tk - 1:
            s = jnp.where((key + diag) <= (qry + q0), s, NEG_BIG)
        return s, jnp.max(s, axis=0, keepdims=True)

    def accumulate(s, tile_max, m, l, acc, v):
        m_new = jnp.maximum(m, tile_max)
        alpha = jnp.exp2(m - m_new)
        p = jnp.exp2(s - m_new)
        l = alpha * l + jnp.sum(p, axis=0, keepdims=True)
        acc = alpha * acc + _tn_dot(v, p.astype(BF16))
        return m_new, l, acc

    def step(kb_next, kb_pending, diag, buf_in, buf_out, carry):
        m, l, acc, tile_max = carry
        k = k_ref[pl.ds(pl.multiple_of(kb_next * tk, tk), tk), :]
        v = v_ref[pl.ds(pl.multiple_of(kb_pending * tk, tk), tk), :]
        m_n, l_n, acc_n, tile_max_n = [], [], [], []
        for c, cols in enumerate(chunks):
            scored = scores(cols, k, diag)
            if scored is not None:
                buf_out[:, cols] = scored[0]
            state = (m[c], l[c], acc[c])
            if tile_max[c] is not None:
                state = accumulate(buf_in[:, cols], tile_max[c], *state, v)
            m_n.append(state[0])
            l_n.append(state[1])
            acc_n.append(state[2])
            tile_max_n.append(None if scored is None else scored[1])
        return m_n, l_n, acc_n, tile_max_n

    per_q = tq // tk
    assert per_q == 2
    d0, d1 = qi * per_q, qi * per_q + 1

    def begin():
        kd1 = k_ref[pl.ds(pl.multiple_of(d1 * tk, tk), tk), :]
        tile_max = []
        for cols in chunks:
            scored = scores(cols, kd1, tk)
            if scored is not None:
                buf1[:, cols] = scored[0]
            tile_max.append(None if scored is None else scored[1])
        n_c = len(chunks)
        carry = ([jnp.full((1, cc), NEG_BIG, F32)] * n_c, [jnp.zeros((1, cc), F32)] * n_c,
                 [jnp.zeros((LANES, cc), F32)] * n_c, tile_max)
        return step(d0, d1, 0, buf1, buf0, carry)

    def pair(j, carry):
        kb = d0 - 1 - 2 * j
        carry = step(kb, kb + 1, None, buf0, buf1, carry)
        return step(kb - 1, kb, None, buf1, buf0, carry)

    def walk(carry):
        return lax.fori_loop(0, qi, pair, carry)

    def finish(carry):
        m, l, acc, tile_max = carry
        v_last = v_ref[pl.ds(0, tk), :]
        outs = [accumulate(buf0[:, cols], tile_max[c], m[c], l[c], acc[c], v_last)
                for c, cols in enumerate(chunks)]
        o = jnp.concatenate([oc[2] * (1.0 / oc[1]) for oc in outs], axis=1)

        lp = lam_ref[...]
        e1 = jnp.exp(jnp.sum(lp[0:1, :] * lp[1:2, :], axis=1, keepdims=True))
        e2 = jnp.exp(jnp.sum(lp[2:3, :] * lp[3:4, :], axis=1, keepdims=True))
        lam = e1 - e2 + lam_init
        o = o[:, :tq] - lam * o[:, tq:]
        ms = jnp.mean(o * o, axis=0, keepdims=True)
        o = o * lax.rsqrt(ms + NORM_EPS) * subln_ref[...] * (1.0 - lam_init)
        return o.T.astype(BF16)

    return begin, walk, finish


def _sb_tile(qi, q, k_ref, v_ref, buf0, buf1):
    tq, tk, cc = SB_TQ, SB_TK, SB_CC
    lane = lax.broadcasted_iota(jnp.int32, q.shape, 1)
    zero = jnp.zeros_like(q)
    per_q = tq // tk
    qs = jnp.concatenate([jnp.where(lane < HEAD_DIM, q, zero),
                          jnp.where(lane >= HEAD_DIM, q, zero)], axis=0)
    ss = lax.broadcasted_iota(jnp.int32, (tk, tk), 0)
    jj = lax.broadcasted_iota(jnp.int32, (tk, tk), 1)
    neg_suffix = jnp.where(jj > ss, -1.0, 0.0).astype(BF16)
    sign_bit = jnp.uint32(0x80000000)
    chunks = [slice(c * cc, (c + 1) * cc) for c in range(2 * tq // cc)]
    key = lax.broadcasted_iota(jnp.int32, (tk, cc), 0)
    qry = lax.broadcasted_iota(jnp.int32, (tk, cc), 1)

    def scores(cols, k, diag):
        q0 = cols.start % tq
        if diag is not None and q0 + cc <= diag + 1:
            return None, jnp.zeros((1, cc), F32)
        if diag is not None and q0 >= diag + tk:
            diag = None
        z = _nt_dot(k, qs[cols])
        neg_abs = lax.bitcast_convert_type(
            lax.bitcast_convert_type(z, jnp.uint32) | sign_bit, F32)
        sp = jnp.maximum(z, 0.0) + jnp.log2(1.0 + jnp.exp2(neg_abs))
        if diag is not None:
            mask = (key + diag) < (qry + q0)
            sp = jnp.where(mask, sp, 0.0)
        arg = (z - sp) + jnp.dot(neg_suffix, sp.astype(BF16), preferred_element_type=F32)
        if diag is not None:
            arg = jnp.where(mask, arg, NEG_BIG)
        return arg, jnp.sum(sp, axis=0, keepdims=True)

    def accumulate(arg, rem, acc, v):
        a = jnp.exp2(arg + rem)
        return acc + _tn_dot(v, a.astype(BF16))

    def step(kb_next, kb_pending, diag, buf_in, buf_out, carry, pending_empty=(),
             exists=None):
        rem, acc, key_sum = carry
        k = k_ref[pl.ds(pl.multiple_of(kb_next * tk, tk), tk), :]
        v = v_ref[pl.ds(pl.multiple_of(kb_pending * tk, tk), tk), :]
        if exists is not None:
            exists_row = jnp.full((1, cc), exists, jnp.int32) > 0
        rem_n, acc_n, key_sum_n = [], [], []
        for c, cols in enumerate(chunks):
            arg_next, ks = scores(cols, k, diag)
            if exists is not None:
                arg_next = jnp.where(exists_row, arg_next, NEG_BIG)
                ks = jnp.where(exists_row, ks, 0.0)
            buf_out[:, cols] = arg_next
            acc_n.append(acc[c] if c in pending_empty
                         else accumulate(buf_in[:, cols], rem[c], acc[c], v))
            rem_n.append(rem[c] - key_sum[c])
            key_sum_n.append(ks)
        return rem_n, acc_n, key_sum_n

    def next_live(carry):
        rem, _, key_sum = carry
        top = functools.reduce(jnp.maximum, [r - ks for r, ks in zip(rem, key_sum)])
        return (jnp.max(top) > SB_DEAD_LOG2).astype(jnp.int32)

    assert per_q == 2
    d1, d0 = qi * per_q + 1, qi * per_q
    f1 = jnp.maximum(d0 - 1, 0)

    def begin():
        k1 = k_ref[pl.ds(pl.multiple_of(d1 * tk, tk), tk), :]
        key_sum, empty = [], []
        for c, cols in enumerate(chunks):
            arg, ks = scores(cols, k1, tk)
            if arg is None:
                empty.append(c)
            else:
                buf0[:, cols] = arg
            key_sum.append(ks)
        carry = ([jnp.zeros((1, cc), F32)] * len(chunks),
                 [jnp.zeros((LANES, cc), F32)] * len(chunks), key_sum)
        carry = step(d0, d1, 0, buf0, buf1, carry, pending_empty=empty)
        return step(f1, d0, None, buf1, buf0, carry, exists=jnp.minimum(qi, 1))

    def pair(state):
        j, _, carry = state
        kb = f1 - 1 - 2 * j
        carry = step(kb, kb + 1, None, buf0, buf1, carry)
        carry = step(jnp.maximum(kb - 1, 0), kb, None, buf1, buf0, carry,
                     exists=jnp.clip(kb, 0, 1))
        return j + 1, next_live(carry), carry

    def walk(carry):
        n_pairs, _, carry = lax.while_loop(
            lambda st: (st[0] < qi) & (st[1] > 0), pair,
            (jnp.int32(0), next_live(carry), carry))
        return n_pairs, carry

    def finish(state):
        n_pairs, (rem, acc, _) = state
        pending = jnp.maximum(f1 - 2 * n_pairs, 0)
        v = v_ref[pl.ds(pl.multiple_of(pending * tk, tk), tk), :]
        acc = jnp.concatenate([accumulate(buf0[:, cols], rem[c], acc[c], v)
                               for c, cols in enumerate(chunks)], axis=1)
        feat = lax.broadcasted_iota(jnp.int32, (LANES, tq), 0)
        return jnp.where(feat < HEAD_DIM, acc[:, :tq], acc[:, tq:]).T.astype(BF16)

    return begin, walk, finish


def _attn_kernel(dq_ref, dk_ref, dv_ref, lam_ref, subln_ref, sq_ref, sk_ref, sv_ref,
                 od_ref, os_ref, dbuf0, dbuf1, sbuf00, sbuf01, sbuf10, sbuf11, *, lam_init):
    ratio = DIFF_TQ // SB_TQ
    sbufs = ((sbuf00, sbuf01), (sbuf10, sbuf11))
    assert ratio == len(sbufs)

    def q_group(g, carry):
        d_rows = pl.ds(pl.multiple_of(g * DIFF_TQ, DIFF_TQ), DIFF_TQ)
        s_rows = [pl.ds(pl.multiple_of((ratio * g + i) * SB_TQ, SB_TQ), SB_TQ)
                  for i in range(ratio)]
        tiles = [_diff_tile(g, dq_ref[d_rows, :], dk_ref, dv_ref, lam_ref, subln_ref,
                            dbuf0, dbuf1, lam_init)]
        tiles += [_sb_tile(ratio * g + i, sq_ref[s_rows[i], :], sk_ref, sv_ref, *sbufs[i])
                  for i in range(ratio)]
        states = [begin() for begin, _, _ in tiles]
        states = [walk(st) for (_, walk, _), st in zip(tiles, states)]
        outs = [finish(st) for (_, _, finish), st in zip(tiles, states)]
        od_ref[d_rows, :] = outs[0]
        for i in range(ratio):
            os_ref[s_rows[i], :] = outs[1 + i]
        return carry

    lax.fori_loop(0, dq_ref.shape[0] // DIFF_TQ, q_group, 0)


def _attention(proj, lam_p, subln, lam_init):
    b, s, _ = proj.shape
    assert DIFF_HEADS == SB_PAIRS
    col = lambda first: pl.BlockSpec((None, s, LANES), lambda bi, h: (bi, 0, first + h))
    dq, dk, dv = (i * DIFF_WIDTH // LANES for i in range(3))
    sq, sk, sv = (3 * DIFF_WIDTH // LANES + i * SB_PAIRS for i in range(3))
    return pl.pallas_call(
        functools.partial(_attn_kernel, lam_init=lam_init),
        grid=(b, DIFF_HEADS),
        in_specs=[
            col(dq), col(dk), col(dv),
            pl.BlockSpec((4, HEAD_DIM), lambda bi, h: (0, 0)),
            pl.BlockSpec((LANES, 1), lambda bi, h: (0, 0)),
            col(sq), col(sk), col(sv),
        ],
        out_specs=[col(0), col(0)],
        out_shape=[jax.ShapeDtypeStruct((b, s, DIFF_WIDTH), BF16),
                   jax.ShapeDtypeStruct((b, s, SB_WIDTH), BF16)],
        scratch_shapes=([pltpu.VMEM((DIFF_TK, 2 * DIFF_TQ), F32)] * 2
                        + [pltpu.VMEM((SB_TK, 2 * SB_TQ), F32)] * 4),
        compiler_params=pltpu.CompilerParams(
            dimension_semantics=("arbitrary", "arbitrary"), vmem_limit_bytes=ATTN_VMEM_LIMIT),
        name="attention",
    )(proj, proj, proj, lam_p, subln, proj, proj, proj)


def _merge_ffn_kernel(x_ref, od_ref, os_ref, gd_ref, gs_ref, wod_ref, wos_ref, wo_ref,
                      g_ref, wi_ref, wd_ref, gf_ref, o_ref, *, final_norm):
    pd = jnp.dot(od_ref[...], wod_ref[...], preferred_element_type=F32)
    ps = jnp.dot(os_ref[...], wos_ref[...], preferred_element_type=F32)
    merged = gd_ref[...].astype(F32) * pd + gs_ref[...].astype(F32) * ps
    x = x_ref[...] + jnp.dot(merged.astype(BF16), wo_ref[...], preferred_element_type=F32)

    ms = jnp.mean(x * x, axis=-1, keepdims=True)
    h = (x * lax.rsqrt(ms + NORM_EPS) * g_ref[...]).astype(BF16)
    y = x
    for c0 in range(0, FFN_HIDDEN, FFN_CH):
        c1 = min(c0 + FFN_CH, FFN_HIDDEN)
        gate = jnp.dot(h, wi_ref[:, c0:c1], preferred_element_type=F32)
        up = jnp.dot(h, wi_ref[:, FFN_HIDDEN + c0:FFN_HIDDEN + c1],
                     preferred_element_type=F32)
        act = (gate * jax.nn.sigmoid(gate) * up).astype(BF16)
        y = y + jnp.dot(act, wd_ref[c0:c1, :], preferred_element_type=F32)
    if final_norm:
        ms = jnp.mean(y * y, axis=-1, keepdims=True)
        y = y * lax.rsqrt(ms + NORM_EPS) * gf_ref[...]
    o_ref[...] = y


def _merge_ffn(x2d, o_diff, o_sb, proj2d, w_o_diff, w_o_sb, w_out, g, w_in, w_down, g_final,
               final_norm):
    t_rows = x2d.shape[0]
    tm = FFN_TM
    gd_blk = GATE_COL0 // D_MODEL
    once = pl.Buffered(1)
    rows = lambda width, blk=0: pl.BlockSpec((tm, width), lambda i: (i, blk))
    whole = lambda shape: pl.BlockSpec(shape, lambda i: (0, 0), pipeline_mode=once)
    return pl.pallas_call(
        functools.partial(_merge_ffn_kernel, final_norm=final_norm),
        grid=(t_rows // tm,),
        in_specs=[
            rows(D_MODEL), rows(DIFF_WIDTH), rows(SB_WIDTH),
            rows(D_MODEL, gd_blk), rows(D_MODEL, gd_blk + 1),
            whole((DIFF_WIDTH, D_MODEL)), whole((SB_WIDTH, D_MODEL)),
            whole((D_MODEL, D_MODEL)), whole((1, D_MODEL)),
            whole((D_MODEL, 2 * FFN_HIDDEN)), whole((FFN_HIDDEN, D_MODEL)),
            whole((1, D_MODEL)),
        ],
        out_specs=rows(D_MODEL),
        out_shape=jax.ShapeDtypeStruct((t_rows, D_MODEL), F32),
        compiler_params=pltpu.CompilerParams(
            dimension_semantics=("arbitrary",), vmem_limit_bytes=VMEM_LIMIT),
        name="merge_ffn",
    )(x2d, o_diff, o_sb, proj2d, proj2d, w_o_diff, w_o_sb, w_out, g, w_in, w_down, g_final)


def _rope_tables(seq):
    half = HEAD_DIM // 2
    pos = jnp.arange(seq, dtype=F32)
    inv = ROPE_THETA ** (-jnp.arange(0, HEAD_DIM, 2, dtype=F32) / HEAD_DIM)
    ang = pos[:, None] * inv[None, :]
    cos, sin = jnp.cos(ang), jnp.sin(ang)
    cos_t = jnp.tile(cos, (1, LANES // half))
    sin_t = jnp.tile(jnp.concatenate([-sin, sin], axis=1), (1, LANES // HEAD_DIM))
    return cos_t, sin_t


def kernel(x, norm_attn, w_in, b_gate, diff_lambda, diff_subln, w_o_diff, w_o_sb, w_out,
           norm_ffn, w_ffn_in, w_ffn_out, norm_final):
    b, s, d = x.shape
    cos_t, sin_t = _rope_tables(s)
    x2d = x.reshape(b * s, d)
    g_final = norm_final.reshape(1, d)
    for layer in range(DEPTH):
        lam_init = 0.8 - 0.6 * math.exp(-0.3 * layer)
        proj2d = _in_proj(x2d, norm_attn[layer].reshape(1, d), w_in[layer].astype(BF16),
                          b_gate[layer].reshape(1, 2 * d), cos_t, sin_t, s)
        proj = proj2d.reshape(b, s, IN_COLS)
        o_diff, o_sb = _attention(proj, diff_lambda[layer],
                                  diff_subln[layer].reshape(LANES, 1), lam_init)
        x2d = _merge_ffn(
            x2d, o_diff.reshape(b * s, DIFF_WIDTH), o_sb.reshape(b * s, SB_WIDTH), proj2d,
            w_o_diff[layer].astype(BF16), w_o_sb[layer].astype(BF16),
            w_out[layer].astype(BF16), norm_ffn[layer].reshape(1, d),
            w_ffn_in[layer].astype(BF16), w_ffn_out[layer].astype(BF16), g_final,
            layer == DEPTH - 1)
    return x2d.reshape(b, s, d)
```

```python
import functools
import math

import jax
import jax.numpy as jnp
from jax import lax
from jax.experimental import pallas as pl
from jax.experimental.pallas import tpu as pltpu

D_MODEL = 1024
DEPTH = 2
HEAD_DIM = 64
LANES = 128
DIFF_HEADS = 4
DIFF_WIDTH = DIFF_HEADS * 2 * HEAD_DIM
SB_HEADS = 8
SB_WIDTH = SB_HEADS * HEAD_DIM
SB_PAIRS = SB_WIDTH // LANES
IN_COLS = 3 * DIFF_WIDTH + 3 * SB_WIDTH + 2 * D_MODEL
GATE_COL0 = 3 * DIFF_WIDTH + 3 * SB_WIDTH
FFN_HIDDEN = 2816
ROPE_THETA = 10000.0
NORM_EPS = 1e-6
NEG_BIG = -1e30
LOG2E = math.log2(math.e)
SB_DEAD_LOG2 = -160.0

VMEM_LIMIT = 48 * 1024 * 1024
ATTN_VMEM_LIMIT = 56 * 1024 * 1024

IN_TM, IN_CW = 1024, 512
DIFF_TQ, DIFF_TK, DIFF_CC = 1024, 512, 256
SB_TQ, SB_TK, SB_CC = 512, 256, 256
FFN_TM, FFN_CH = 512, 512

BF16 = jnp.bfloat16
F32 = jnp.float32


def _nt_dot(a, b):
    return lax.dot_general(a, b, (((1,), (1,)), ((), ())), preferred_element_type=F32)


def _in_proj_kernel(x_ref, g_ref, w_ref, bg_ref, cos_ref, sin_ref, o_ref):
    x = x_ref[...]
    ms = jnp.mean(x * x, axis=-1, keepdims=True)
    h = (x * lax.rsqrt(ms + NORM_EPS) * g_ref[...]).astype(BF16)

    scale = HEAD_DIM ** -0.5 * LOG2E
    cw = IN_CW
    assert DIFF_WIDTH % cw == 0 and SB_WIDTH % cw == 0
    cos = cos_ref[...]
    sin = sin_ref[...]
    lane = lax.broadcasted_iota(jnp.int32, cos.shape, 1)
    first_half = (lane % HEAD_DIM) < (HEAD_DIM // 2)

    for c0 in range(0, IN_COLS, cw):
        acc = jnp.dot(h, w_ref[:, c0:c0 + cw], preferred_element_type=F32)
        is_q = c0 < DIFF_WIDTH or 3 * DIFF_WIDTH <= c0 < 3 * DIFF_WIDTH + SB_WIDTH
        if c0 < 2 * DIFF_WIDTH:
            for b0 in range(0, cw, LANES):
                t = acc[:, b0:b0 + LANES]
                partner = jnp.where(first_half,
                                    pltpu.roll(t, LANES - HEAD_DIM // 2, 1),
                                    pltpu.roll(t, HEAD_DIM // 2, 1))
                r = t * cos + partner * sin
                if is_q:
                    r = r * scale
                o_ref[:, c0 + b0:c0 + b0 + LANES] = r.astype(BF16)
        elif c0 < GATE_COL0:
            if is_q:
                acc = acc * scale
            o_ref[:, c0:c0 + cw] = acc.astype(BF16)
        else:
            g0 = c0 - GATE_COL0
            o_ref[:, c0:c0 + cw] = jax.nn.sigmoid(acc + bg_ref[:, g0:g0 + cw]).astype(BF16)


def _in_proj(x2d, g, w, bg, cos_t, sin_t, seq):
    t_rows = x2d.shape[0]
    pos_tiles = seq // IN_TM
    return pl.pallas_call(
        _in_proj_kernel,
        grid=(t_rows // IN_TM,),
        in_specs=[
            pl.BlockSpec((IN_TM, D_MODEL), lambda i: (i, 0)),
            pl.BlockSpec((1, D_MODEL), lambda i: (0, 0)),
            pl.BlockSpec((D_MODEL, IN_COLS), lambda i: (0, 0), pipeline_mode=pl.Buffered(1)),
            pl.BlockSpec((1, 2 * D_MODEL), lambda i: (0, 0)),
            pl.BlockSpec((IN_TM, LANES), lambda i: (i % pos_tiles, 0)),
            pl.BlockSpec((IN_TM, LANES), lambda i: (i % pos_tiles, 0)),
        ],
        out_specs=pl.BlockSpec((IN_TM, IN_COLS), lambda i: (i, 0)),
        out_shape=jax.ShapeDtypeStruct((t_rows, IN_COLS), BF16),
        compiler_params=pltpu.CompilerParams(
            dimension_semantics=("arbitrary",), vmem_limit_bytes=VMEM_LIMIT),
        name="in_proj",
    )(x2d, g, w, bg, cos_t, sin_t)


def _tn_dot(a, b):
    return lax.dot_general(a, b, (((0,), (0,)), ((), ())), preferred_element_type=F32)


def _diff_tile(qi, q, k_ref, v_ref, lam_ref, subln_ref, buf0, buf1, lam_init):
    tq, tk, cc = DIFF_TQ, DIFF_TK, DIFF_CC
    lane = lax.broadcasted_iota(jnp.int32, q.shape, 1)
    zero = jnp.zeros_like(q)
    qs = jnp.concatenate([jnp.where(lane < HEAD_DIM, q, zero),
                          jnp.where(lane >= HEAD_DIM, q, zero)], axis=0)
    chunks = [slice(c * cc, (c + 1) * cc) for c in range(2 * tq // cc)]
    key = lax.broadcasted_iota(jnp.int32, (tk, cc), 0)
    qry = lax.broadcasted_iota(jnp.int32, (tk, cc), 1)

    def scores(cols, k, diag):
        q0 = cols.start % tq
        if diag is not None and q0 + cc <= diag:
            return None
        s = _nt_dot(k, qs[cols])
        if diag is not None and q0 < diag + tk - 1:
            s = jnp.where((key + diag) <= (qry + q0), s, NEG_BIG)
        return s, jnp.max(s, axis=0, keepdims=True)

    def accumulate(s, tile_max, m, l, acc, v):
        m_new = jnp.maximum(m, tile_max)
        alpha = jnp.exp2(m - m_new)
        p = jnp.exp2(s - m_new)
        l = alpha * l + jnp.sum(p, axis=0, keepdims=True)
        acc = alpha * acc + _tn_dot(v, p.astype(BF16))
        return m_new, l, acc

    def step(kb_next, kb_pending, diag, buf_in, buf_out, carry):
        m, l, acc, tile_max = carry
        k = k_ref[pl.ds(pl.multiple_of(kb_next * tk, tk), tk), :]
        v = v_ref[pl.ds(pl.multiple_of(kb_pending * tk, tk), tk), :]
        m_n, l_n, acc_n, tile_max_n = [], [], [], []
        for c, cols in enumerate(chunks):
            scored = scores(cols, k, diag)
            if scored is not None:
                buf_out[:, cols] = scored[0]
            state = (m[c], l[c], acc[c])
            if tile_max[c] is not None:
                state = accumulate(buf_in[:, cols], tile_max[c], *state, v)
            m_n.append(state[0])
            l_n.append(state[1])
            acc_n.append(state[2])
            tile_max_n.append(None if scored is None else scored[1])
        return m_n, l_n, acc_n, tile_max_n

    per_q = tq // tk
    assert per_q == 2
    d0, d1 = qi * per_q, qi * per_q + 1

    def begin_scores(_):
        kd1 = k_ref[pl.ds(pl.multiple_of(d1 * tk, tk), tk), :]
        tile_max = []
        for cols in chunks:
            scored = scores(cols, kd1, tk)
            if scored is not None:
                buf1[:, cols] = scored[0]
            tile_max.append(None if scored is None else scored[1])
        n_c = len(chunks)
        return ([jnp.full((1, cc), NEG_BIG, F32)] * n_c, [jnp.zeros((1, cc), F32)] * n_c,
                [jnp.zeros((LANES, cc), F32)] * n_c, tile_max)

    begin = (begin_scores, lambda carry: step(d0, d1, 0, buf1, buf0, carry))

    def pair(j, carry):
        kb = d0 - 1 - 2 * j
        carry = step(kb, kb + 1, None, buf0, buf1, carry)
        return step(kb - 1, kb, None, buf1, buf0, carry)

    def walk(carry):
        return lax.fori_loop(0, qi, pair, carry)

    def finish(carry):
        m, l, acc, tile_max = carry
        v_last = v_ref[pl.ds(0, tk), :]
        outs = [accumulate(buf0[:, cols], tile_max[c], m[c], l[c], acc[c], v_last)
                for c, cols in enumerate(chunks)]
        o = jnp.concatenate([oc[2] * (1.0 / oc[1]) for oc in outs], axis=1)

        lp = lam_ref[...]
        e1 = jnp.exp(jnp.sum(lp[0:1, :] * lp[1:2, :], axis=1, keepdims=True))
        e2 = jnp.exp(jnp.sum(lp[2:3, :] * lp[3:4, :], axis=1, keepdims=True))
        lam = e1 - e2 + lam_init
        o = o[:, :tq] - lam * o[:, tq:]
        ms = jnp.mean(o * o, axis=0, keepdims=True)
        o = o * lax.rsqrt(ms + NORM_EPS) * subln_ref[...] * (1.0 - lam_init)
        return o.T.astype(BF16)

    return begin, walk, finish


def _sb_tile(qi, q, k_ref, v_ref, buf0, buf1):
    tq, tk, cc = SB_TQ, SB_TK, SB_CC
    lane = lax.broadcasted_iota(jnp.int32, q.shape, 1)
    zero = jnp.zeros_like(q)
    per_q = tq // tk
    qs = jnp.concatenate([jnp.where(lane < HEAD_DIM, q, zero),
                          jnp.where(lane >= HEAD_DIM, q, zero)], axis=0)
    ss = lax.broadcasted_iota(jnp.int32, (tk, tk), 0)
    jj = lax.broadcasted_iota(jnp.int32, (tk, tk), 1)
    neg_suffix = jnp.where(jj > ss, -1.0, 0.0).astype(BF16)
    sign_bit = jnp.uint32(0x80000000)
    chunks = [slice(c * cc, (c + 1) * cc) for c in range(2 * tq // cc)]
    key = lax.broadcasted_iota(jnp.int32, (tk, cc), 0)
    qry = lax.broadcasted_iota(jnp.int32, (tk, cc), 1)

    def scores(cols, k, diag):
        q0 = cols.start % tq
        if diag is not None and q0 + cc <= diag + 1:
            return None, jnp.zeros((1, cc), F32)
        if diag is not None and q0 >= diag + tk:
            diag = None
        z = _nt_dot(k, qs[cols])
        neg_abs = lax.bitcast_convert_type(
            lax.bitcast_convert_type(z, jnp.uint32) | sign_bit, F32)
        sp = jnp.maximum(z, 0.0) + jnp.log2(1.0 + jnp.exp2(neg_abs))
        if diag is not None:
            mask = (key + diag) < (qry + q0)
            sp = jnp.where(mask, sp, 0.0)
        arg = (z - sp) + jnp.dot(neg_suffix, sp.astype(BF16), preferred_element_type=F32)
        if diag is not None:
            arg = jnp.where(mask, arg, NEG_BIG)
        return arg, jnp.sum(sp, axis=0, keepdims=True)

    def accumulate(arg, rem, acc, v):
        a = jnp.exp2(arg + rem)
        return acc + _tn_dot(v, a.astype(BF16))

    def step(kb_next, kb_pending, diag, buf_in, buf_out, carry, pending_empty=(),
             exists=None):
        rem, acc, key_sum = carry
        k = k_ref[pl.ds(pl.multiple_of(kb_next * tk, tk), tk), :]
        v = v_ref[pl.ds(pl.multiple_of(kb_pending * tk, tk), tk), :]
        if exists is not None:
            exists_row = jnp.full((1, cc), exists, jnp.int32) > 0
        rem_n, acc_n, key_sum_n = [], [], []
        for c, cols in enumerate(chunks):
            arg_next, ks = scores(cols, k, diag)
            buf_out[:, cols] = arg_next
            acc_n.append(acc[c] if c in pending_empty
                         else accumulate(buf_in[:, cols], rem[c], acc[c], v))
            rem_next = rem[c] - key_sum[c]
            if exists is not None:
                rem_next = jnp.where(exists_row, rem_next, NEG_BIG)
                ks = jnp.where(exists_row, ks, 0.0)
            rem_n.append(rem_next)
            key_sum_n.append(ks)
        return rem_n, acc_n, key_sum_n

    def next_live(carry):
        rem, _, key_sum = carry
        top = functools.reduce(jnp.maximum, [r - ks for r, ks in zip(rem, key_sum)])
        return (jnp.max(top) > SB_DEAD_LOG2).astype(jnp.int32)

    assert per_q == 2
    d1, d0 = qi * per_q + 1, qi * per_q
    f1 = jnp.maximum(d0 - 1, 0)

    empty = []

    def begin_scores(_):
        k1 = k_ref[pl.ds(pl.multiple_of(d1 * tk, tk), tk), :]
        key_sum = []
        for c, cols in enumerate(chunks):
            arg, ks = scores(cols, k1, tk)
            if arg is None:
                empty.append(c)
            else:
                buf0[:, cols] = arg
            key_sum.append(ks)
        return ([jnp.zeros((1, cc), F32)] * len(chunks),
                [jnp.zeros((LANES, cc), F32)] * len(chunks), key_sum)

    begin = (begin_scores,
             lambda carry: step(d0, d1, 0, buf0, buf1, carry, pending_empty=empty),
             lambda carry: step(f1, d0, None, buf1, buf0, carry, exists=jnp.minimum(qi, 1)))

    def pair(state):
        j, _, carry = state
        kb = f1 - 1 - 2 * j
        carry = step(kb, kb + 1, None, buf0, buf1, carry)
        carry = step(jnp.maximum(kb - 1, 0), kb, None, buf1, buf0, carry,
                     exists=jnp.clip(kb, 0, 1))
        return j + 1, next_live(carry), carry

    def walk(carry):
        n_pairs, _, carry = lax.while_loop(
            lambda st: (st[0] < qi) & (st[1] > 0), pair,
            (jnp.int32(0), next_live(carry), carry))
        return n_pairs, carry

    def finish(state):
        n_pairs, (rem, acc, _) = state
        pending = jnp.maximum(f1 - 2 * n_pairs, 0)
        v = v_ref[pl.ds(pl.multiple_of(pending * tk, tk), tk), :]
        acc = jnp.concatenate([accumulate(buf0[:, cols], rem[c], acc[c], v)
                               for c, cols in enumerate(chunks)], axis=1)
        feat = lax.broadcasted_iota(jnp.int32, (LANES, tq), 0)
        return jnp.where(feat < HEAD_DIM, acc[:, :tq], acc[:, tq:]).T.astype(BF16)

    return begin, walk, finish


def _attn_kernel(dq_ref, dk_ref, dv_ref, lam_ref, subln_ref, sq_ref, sk_ref, sv_ref,
                 od_ref, os_ref, dbuf0, dbuf1, sbuf00, sbuf01, sbuf10, sbuf11, *, lam_init):
    ratio = DIFF_TQ // SB_TQ
    sbufs = ((sbuf00, sbuf01), (sbuf10, sbuf11))
    assert ratio == len(sbufs)

    def q_group(g, carry):
        d_rows = pl.ds(pl.multiple_of(g * DIFF_TQ, DIFF_TQ), DIFF_TQ)
        s_rows = [pl.ds(pl.multiple_of((ratio * g + i) * SB_TQ, SB_TQ), SB_TQ)
                  for i in range(ratio)]
        tiles = [_diff_tile(g, dq_ref[d_rows, :], dk_ref, dv_ref, lam_ref, subln_ref,
                            dbuf0, dbuf1, lam_init)]
        tiles += [_sb_tile(ratio * g + i, sq_ref[s_rows[i], :], sk_ref, sv_ref, *sbufs[i])
                  for i in range(ratio)]
        order = list(range(1, len(tiles))) + [0]
        states = [None] * len(tiles)
        for phase in range(max(len(t[0]) for t in tiles)):
            for t in order:
                if phase < len(tiles[t][0]):
                    states[t] = tiles[t][0][phase](states[t])
        states = [walk(st) for (_, walk, _), st in zip(tiles, states)]
        outs = [finish(st) for (_, _, finish), st in zip(tiles, states)]
        od_ref[d_rows, :] = outs[0]
        for i in range(ratio):
            os_ref[s_rows[i], :] = outs[1 + i]
        return carry

    lax.fori_loop(0, dq_ref.shape[0] // DIFF_TQ, q_group, 0)


def _attention(proj, lam_p, subln, lam_init):
    b, s, _ = proj.shape
    assert DIFF_HEADS == SB_PAIRS
    col = lambda first: pl.BlockSpec((None, s, LANES), lambda bi, h: (bi, 0, first + h))
    dq, dk, dv = (i * DIFF_WIDTH // LANES for i in range(3))
    sq, sk, sv = (3 * DIFF_WIDTH // LANES + i * SB_PAIRS for i in range(3))
    return pl.pallas_call(
        functools.partial(_attn_kernel, lam_init=lam_init),
        grid=(b, DIFF_HEADS),
        in_specs=[
            col(dq), col(dk), col(dv),
            pl.BlockSpec((4, HEAD_DIM), lambda bi, h: (0, 0)),
            pl.BlockSpec((LANES, 1), lambda bi, h: (0, 0)),
            col(sq), col(sk), col(sv),
        ],
        out_specs=[col(0), col(0)],
        out_shape=[jax.ShapeDtypeStruct((b, s, DIFF_WIDTH), BF16),
                   jax.ShapeDtypeStruct((b, s, SB_WIDTH), BF16)],
        scratch_shapes=([pltpu.VMEM((DIFF_TK, 2 * DIFF_TQ), F32)] * 2
                        + [pltpu.VMEM((SB_TK, 2 * SB_TQ), F32)] * 4),
        compiler_params=pltpu.CompilerParams(
            dimension_semantics=("arbitrary", "arbitrary"), vmem_limit_bytes=ATTN_VMEM_LIMIT),
        name="attention",
    )(proj, proj, proj, lam_p, subln, proj, proj, proj)


def _merge_ffn_kernel(x_ref, od_ref, os_ref, gd_ref, gs_ref, wod_ref, wos_ref, wo_ref,
                      g_ref, wi_ref, wd_ref, gf_ref, o_ref, *, final_norm):
    pd = jnp.dot(od_ref[...], wod_ref[...], preferred_element_type=F32)
    ps = jnp.dot(os_ref[...], wos_ref[...], preferred_element_type=F32)
    merged = gd_ref[...].astype(F32) * pd + gs_ref[...].astype(F32) * ps
    x = x_ref[...] + jnp.dot(merged.astype(BF16), wo_ref[...], preferred_element_type=F32)

    ms = jnp.mean(x * x, axis=-1, keepdims=True)
    h = (x * lax.rsqrt(ms + NORM_EPS) * g_ref[...]).astype(BF16)
    y = x
    for c0 in range(0, FFN_HIDDEN, FFN_CH):
        c1 = min(c0 + FFN_CH, FFN_HIDDEN)
        gate = jnp.dot(h, wi_ref[:, c0:c1], preferred_element_type=F32)
        up = jnp.dot(h, wi_ref[:, FFN_HIDDEN + c0:FFN_HIDDEN + c1],
                     preferred_element_type=F32)
        act = (gate * jax.nn.sigmoid(gate) * up).astype(BF16)
        y = y + jnp.dot(act, wd_ref[c0:c1, :], preferred_element_type=F32)
    if final_norm:
        ms = jnp.mean(y * y, axis=-1, keepdims=True)
        y = y * lax.rsqrt(ms + NORM_EPS) * gf_ref[...]
    o_ref[...] = y


def _merge_ffn(x2d, o_diff, o_sb, proj2d, w_o_diff, w_o_sb, w_out, g, w_in, w_down, g_final,
               final_norm):
    t_rows = x2d.shape[0]
    tm = FFN_TM
    gd_blk = GATE_COL0 // D_MODEL
    once = pl.Buffered(1)
    rows = lambda width, blk=0: pl.BlockSpec((tm, width), lambda i: (i, blk))
    whole = lambda shape: pl.BlockSpec(shape, lambda i: (0, 0), pipeline_mode=once)
    return pl.pallas_call(
        functools.partial(_merge_ffn_kernel, final_norm=final_norm),
        grid=(t_rows // tm,),
        in_specs=[
            rows(D_MODEL), rows(DIFF_WIDTH), rows(SB_WIDTH),
            rows(D_MODEL, gd_blk), rows(D_MODEL, gd_blk + 1),
            whole((DIFF_WIDTH, D_MODEL)), whole((SB_WIDTH, D_MODEL)),
            whole((D_MODEL, D_MODEL)), whole((1, D_MODEL)),
            whole((D_MODEL, 2 * FFN_HIDDEN)), whole((FFN_HIDDEN, D_MODEL)),
            whole((1, D_MODEL)),
        ],
        out_specs=rows(D_MODEL),
        out_shape=jax.ShapeDtypeStruct((t_rows, D_MODEL), F32),
        compiler_params=pltpu.CompilerParams(
            dimension_semantics=("arbitrary",), vmem_limit_bytes=VMEM_LIMIT),
        name="merge_ffn",
    )(x2d, o_diff, o_sb, proj2d, proj2d, w_o_diff, w_o_sb, w_out, g, w_in, w_down, g_final)


def _rope_tables(seq):
    half = HEAD_DIM // 2
    pos = jnp.arange(seq, dtype=F32)
    inv = ROPE_THETA ** (-jnp.arange(0, HEAD_DIM, 2, dtype=F32) / HEAD_DIM)
    ang = pos[:, None] * inv[None, :]
    cos, sin = jnp.cos(ang), jnp.sin(ang)
    cos_t = jnp.tile(cos, (1, LANES // half))
    sin_t = jnp.tile(jnp.concatenate([-sin, sin], axis=1), (1, LANES // HEAD_DIM))
    return cos_t, sin_t


def kernel(x, norm_attn, w_in, b_gate, diff_lambda, diff_subln, w_o_diff, w_o_sb, w_out,
           norm_ffn, w_ffn_in, w_ffn_out, norm_final):
    b, s, d = x.shape
    cos_t, sin_t = _rope_tables(s)
    x2d = x.reshape(b * s, d)
    g_final = norm_final.reshape(1, d)
    for layer in range(DEPTH):
        lam_init = 0.8 - 0.6 * math.exp(-0.3 * layer)
        proj2d = _in_proj(x2d, norm_attn[layer].reshape(1, d), w_in[layer].astype(BF16),
                          b_gate[layer].reshape(1, 2 * d), cos_t, sin_t, s)
        proj = proj2d.reshape(b, s, IN_COLS)
        o_diff, o_sb = _attention(proj, diff_lambda[layer],
                                  diff_subln[layer].reshape(LANES, 1), lam_init)
        x2d = _merge_ffn(
            x2d, o_diff.reshape(b * s, DIFF_WIDTH), o_sb.reshape(b * s, SB_WIDTH), proj2d,
            w_o_diff[layer].astype(BF16), w_o_sb[layer].astype(BF16),
            w_out[layer].astype(BF16), norm_ffn[layer].reshape(1, d),
            w_ffn_in[layer].astype(BF16), w_ffn_out[layer].astype(BF16), g_final,
            layer == DEPTH - 1)
    return x2d.reshape(b, s, d)
```

```python
import functools
import math

import jax
import jax.numpy as jnp
from jax import lax
from jax.experimental import pallas as pl
from jax.experimental.pallas import tpu as pltpu

D_MODEL = 1024
DEPTH = 2
HEAD_DIM = 64
LANES = 128
DIFF_HEADS = 4
DIFF_WIDTH = DIFF_HEADS * 2 * HEAD_DIM
SB_HEADS = 8
SB_WIDTH = SB_HEADS * HEAD_DIM
SB_PAIRS = SB_WIDTH // LANES
IN_COLS = 3 * DIFF_WIDTH + 3 * SB_WIDTH + 2 * D_MODEL
GATE_COL0 = 3 * DIFF_WIDTH + 3 * SB_WIDTH
FFN_HIDDEN = 2816
ROPE_THETA = 10000.0
NORM_EPS = 1e-6
NEG_BIG = -1e30
LOG2E = math.log2(math.e)
SB_DEAD_LOG2 = -160.0

VMEM_LIMIT = 48 * 1024 * 1024
ATTN_VMEM_LIMIT = 56 * 1024 * 1024

IN_TM, IN_CW = 1024, 512
DIFF_TQ, DIFF_TK, DIFF_CC = 1024, 512, 256
SB_TQ, SB_TK, SB_CC = 512, 256, 256
FFN_TM, FFN_CH = 512, 512

BF16 = jnp.bfloat16
F32 = jnp.float32


def _nt_dot(a, b):
    return lax.dot_general(a, b, (((1,), (1,)), ((), ())), preferred_element_type=F32)


def _in_proj_kernel(x_ref, g_ref, w_ref, bg_ref, cos_ref, sin_ref, o_ref):
    x = x_ref[...]
    ms = jnp.mean(x * x, axis=-1, keepdims=True)
    h = (x * lax.rsqrt(ms + NORM_EPS) * g_ref[...]).astype(BF16)

    scale = HEAD_DIM ** -0.5 * LOG2E
    cw = IN_CW
    assert DIFF_WIDTH % cw == 0 and SB_WIDTH % cw == 0
    cos = cos_ref[...]
    sin = sin_ref[...]
    lane = lax.broadcasted_iota(jnp.int32, cos.shape, 1)
    first_half = (lane % HEAD_DIM) < (HEAD_DIM // 2)

    for c0 in range(0, IN_COLS, cw):
        acc = jnp.dot(h, w_ref[:, c0:c0 + cw], preferred_element_type=F32)
        is_q = c0 < DIFF_WIDTH or 3 * DIFF_WIDTH <= c0 < 3 * DIFF_WIDTH + SB_WIDTH
        if c0 < 2 * DIFF_WIDTH:
            for b0 in range(0, cw, LANES):
                t = acc[:, b0:b0 + LANES]
                partner = jnp.where(first_half,
                                    pltpu.roll(t, LANES - HEAD_DIM // 2, 1),
                                    pltpu.roll(t, HEAD_DIM // 2, 1))
                r = t * cos + partner * sin
                if is_q:
                    r = r * scale
                o_ref[:, c0 + b0:c0 + b0 + LANES] = r.astype(BF16)
        elif c0 < GATE_COL0:
            if is_q:
                acc = acc * scale
            o_ref[:, c0:c0 + cw] = acc.astype(BF16)
        else:
            g0 = c0 - GATE_COL0
            o_ref[:, c0:c0 + cw] = jax.nn.sigmoid(acc + bg_ref[:, g0:g0 + cw]).astype(BF16)


def _in_proj(x2d, g, w_all, layer, bg, cos_t, sin_t, seq):
    t_rows = x2d.shape[0]
    pos_tiles = seq // IN_TM
    return pl.pallas_call(
        _in_proj_kernel,
        grid=(t_rows // IN_TM,),
        in_specs=[
            pl.BlockSpec((IN_TM, D_MODEL), lambda i: (i, 0)),
            pl.BlockSpec((1, D_MODEL), lambda i: (0, 0)),
            pl.BlockSpec((None, D_MODEL, IN_COLS), lambda i: (layer, 0, 0),
                         pipeline_mode=pl.Buffered(1)),
            pl.BlockSpec((1, 2 * D_MODEL), lambda i: (0, 0)),
            pl.BlockSpec((IN_TM, LANES), lambda i: (i % pos_tiles, 0)),
            pl.BlockSpec((IN_TM, LANES), lambda i: (i % pos_tiles, 0)),
        ],
        out_specs=pl.BlockSpec((IN_TM, IN_COLS), lambda i: (i, 0)),
        out_shape=jax.ShapeDtypeStruct((t_rows, IN_COLS), BF16),
        compiler_params=pltpu.CompilerParams(
            dimension_semantics=("arbitrary",), vmem_limit_bytes=VMEM_LIMIT),
        name="in_proj",
    )(x2d, g, w_all, bg, cos_t, sin_t)


def _tn_dot(a, b):
    return lax.dot_general(a, b, (((0,), (0,)), ((), ())), preferred_element_type=F32)


def _diff_tile(qi, q, k_ref, v_ref, lam_ref, subln_ref, buf0, buf1, lam_init):
    tq, tk, cc = DIFF_TQ, DIFF_TK, DIFF_CC
    lane = lax.broadcasted_iota(jnp.int32, q.shape, 1)
    zero = jnp.zeros_like(q)
    qs = jnp.concatenate([jnp.where(lane < HEAD_DIM, q, zero),
                          jnp.where(lane >= HEAD_DIM, q, zero)], axis=0)
    chunks = [slice(c * cc, (c + 1) * cc) for c in range(2 * tq // cc)]
    key = lax.broadcasted_iota(jnp.int32, (tk, cc), 0)
    qry = lax.broadcasted_iota(jnp.int32, (tk, cc), 1)

    def scores(cols, k, diag):
        q0 = cols.start % tq
        if diag is not None and q0 + cc <= diag:
            return None
        s = _nt_dot(k, qs[cols])
        if diag is not None and q0 < diag + tk - 1:
            s = jnp.where((key + diag) <= (qry + q0), s, NEG_BIG)
        return s, jnp.max(s, axis=0, keepdims=True)

    def accumulate(s, tile_max, m, l, acc, v):
        m_new = jnp.maximum(m, tile_max)
        alpha = jnp.exp2(m - m_new)
        p = jnp.exp2(s - m_new)
        l = alpha * l + jnp.sum(p, axis=0, keepdims=True)
        acc = alpha * acc + _tn_dot(v, p.astype(BF16))
        return m_new, l, acc

    def step(kb_next, kb_pending, diag, buf_in, buf_out, carry):
        m, l, acc, tile_max = carry
        k = k_ref[pl.ds(pl.multiple_of(kb_next * tk, tk), tk), :]
        v = v_ref[pl.ds(pl.multiple_of(kb_pending * tk, tk), tk), :]
        m_n, l_n, acc_n, tile_max_n = [], [], [], []
        for c, cols in enumerate(chunks):
            scored = scores(cols, k, diag)
            if scored is not None:
                buf_out[:, cols] = scored[0]
            state = (m[c], l[c], acc[c])
            if tile_max[c] is not None:
                state = accumulate(buf_in[:, cols], tile_max[c], *state, v)
            m_n.append(state[0])
            l_n.append(state[1])
            acc_n.append(state[2])
            tile_max_n.append(None if scored is None else scored[1])
        return m_n, l_n, acc_n, tile_max_n

    per_q = tq // tk
    assert per_q == 2
    d0, d1 = qi * per_q, qi * per_q + 1

    def begin_scores(_):
        kd1 = k_ref[pl.ds(pl.multiple_of(d1 * tk, tk), tk), :]
        tile_max = []
        for cols in chunks:
            scored = scores(cols, kd1, tk)
            if scored is not None:
                buf1[:, cols] = scored[0]
            tile_max.append(None if scored is None else scored[1])
        n_c = len(chunks)
        return ([jnp.full((1, cc), NEG_BIG, F32)] * n_c, [jnp.zeros((1, cc), F32)] * n_c,
                [jnp.zeros((LANES, cc), F32)] * n_c, tile_max)

    begin = (begin_scores, lambda carry: step(d0, d1, 0, buf1, buf0, carry))

    def pair(j, carry):
        kb = d0 - 1 - 2 * j
        carry = step(kb, kb + 1, None, buf0, buf1, carry)
        return step(kb - 1, kb, None, buf1, buf0, carry)

    def walk(carry):
        return lax.fori_loop(0, qi, pair, carry)

    def finish(carry):
        m, l, acc, tile_max = carry
        v_last = v_ref[pl.ds(0, tk), :]
        outs = [accumulate(buf0[:, cols], tile_max[c], m[c], l[c], acc[c], v_last)
                for c, cols in enumerate(chunks)]
        o = jnp.concatenate([oc[2] * (1.0 / oc[1]) for oc in outs], axis=1)

        lp = lam_ref[...]
        e1 = jnp.exp(jnp.sum(lp[0:1, :] * lp[1:2, :], axis=1, keepdims=True))
        e2 = jnp.exp(jnp.sum(lp[2:3, :] * lp[3:4, :], axis=1, keepdims=True))
        lam = e1 - e2 + lam_init
        o = o[:, :tq] - lam * o[:, tq:]
        ms = jnp.mean(o * o, axis=0, keepdims=True)
        o = o * lax.rsqrt(ms + NORM_EPS) * subln_ref[...] * (1.0 - lam_init)
        return o.T.astype(BF16)

    return begin, walk, finish


def _sb_tile(qi, q, k_ref, v_ref, buf0, buf1):
    tq, tk, cc = SB_TQ, SB_TK, SB_CC
    lane = lax.broadcasted_iota(jnp.int32, q.shape, 1)
    zero = jnp.zeros_like(q)
    per_q = tq // tk
    qs = jnp.concatenate([jnp.where(lane < HEAD_DIM, q, zero),
                          jnp.where(lane >= HEAD_DIM, q, zero)], axis=0)
    ss = lax.broadcasted_iota(jnp.int32, (tk, tk), 0)
    jj = lax.broadcasted_iota(jnp.int32, (tk, tk), 1)
    neg_suffix = jnp.where(jj > ss, -1.0, 0.0).astype(BF16)
    chunks = [slice(c * cc, (c + 1) * cc) for c in range(2 * tq // cc)]
    key = lax.broadcasted_iota(jnp.int32, (tk, cc), 0)
    qry = lax.broadcasted_iota(jnp.int32, (tk, cc), 1)

    def scores(cols, k, diag):
        q0 = cols.start % tq
        if diag is not None and q0 + cc <= diag + 1:
            return None, jnp.zeros((1, cc), F32)
        if diag is not None and q0 >= diag + tk:
            diag = None
        z = _nt_dot(k, qs[cols])
        sp = jnp.maximum(z, 0.0) + jnp.log2(1.0 + jnp.exp2(-jnp.abs(z)))
        if diag is not None:
            mask = (key + diag) < (qry + q0)
            sp = jnp.where(mask, sp, 0.0)
        arg = (z - sp) + jnp.dot(neg_suffix, sp.astype(BF16), preferred_element_type=F32)
        if diag is not None:
            arg = jnp.where(mask, arg, NEG_BIG)
        return arg, jnp.sum(sp, axis=0, keepdims=True)

    def accumulate(arg, rem, acc, v):
        a = jnp.exp2(arg + rem)
        return acc + _tn_dot(v, a.astype(BF16))

    def step(kb_next, kb_pending, diag, buf_in, buf_out, carry, pending_empty=(),
             exists=None):
        rem, acc, key_sum = carry
        k = k_ref[pl.ds(pl.multiple_of(kb_next * tk, tk), tk), :]
        v = v_ref[pl.ds(pl.multiple_of(kb_pending * tk, tk), tk), :]
        if exists is not None:
            exists_row = jnp.full((1, cc), exists, jnp.int32) > 0
        rem_n, acc_n, key_sum_n = [], [], []
        for c, cols in enumerate(chunks):
            arg_next, ks = scores(cols, k, diag)
            buf_out[:, cols] = arg_next
            acc_n.append(acc[c] if c in pending_empty
                         else accumulate(buf_in[:, cols], rem[c], acc[c], v))
            rem_next = rem[c] - key_sum[c]
            if exists is not None:
                rem_next = jnp.where(exists_row, rem_next, NEG_BIG)
                ks = jnp.where(exists_row, ks, 0.0)
            rem_n.append(rem_next)
            key_sum_n.append(ks)
        return rem_n, acc_n, key_sum_n

    def next_live(carry):
        rem, _, key_sum = carry
        top = functools.reduce(jnp.maximum, [r - ks for r, ks in zip(rem, key_sum)])
        return (jnp.max(top) > SB_DEAD_LOG2).astype(jnp.int32)

    assert per_q == 2
    d1, d0 = qi * per_q + 1, qi * per_q
    f1 = jnp.maximum(d0 - 1, 0)

    empty = []

    def begin_scores(_):
        k1 = k_ref[pl.ds(pl.multiple_of(d1 * tk, tk), tk), :]
        key_sum = []
        for c, cols in enumerate(chunks):
            arg, ks = scores(cols, k1, tk)
            if arg is None:
                empty.append(c)
            else:
                buf0[:, cols] = arg
            key_sum.append(ks)
        return ([jnp.zeros((1, cc), F32)] * len(chunks),
                [jnp.zeros((LANES, cc), F32)] * len(chunks), key_sum)

    begin = (begin_scores,
             lambda carry: step(d0, d1, 0, buf0, buf1, carry, pending_empty=empty),
             lambda carry: step(f1, d0, None, buf1, buf0, carry, exists=jnp.minimum(qi, 1)))

    def pair(state):
        j, _, carry = state
        kb = f1 - 1 - 2 * j
        carry = step(kb, kb + 1, None, buf0, buf1, carry)
        carry = step(jnp.maximum(kb - 1, 0), kb, None, buf1, buf0, carry,
                     exists=jnp.clip(kb, 0, 1))
        return j + 1, next_live(carry), carry

    def walk(carry):
        n_pairs, _, carry = lax.while_loop(
            lambda st: (st[0] < qi) & (st[1] > 0), pair,
            (jnp.int32(0), next_live(carry), carry))
        return n_pairs, carry

    def finish(state):
        n_pairs, (rem, acc, _) = state
        pending = jnp.maximum(f1 - 2 * n_pairs, 0)
        v = v_ref[pl.ds(pl.multiple_of(pending * tk, tk), tk), :]
        acc = jnp.concatenate([accumulate(buf0[:, cols], rem[c], acc[c], v)
                               for c, cols in enumerate(chunks)], axis=1)
        feat = lax.broadcasted_iota(jnp.int32, (LANES, tq), 0)
        return jnp.where(feat < HEAD_DIM, acc[:, :tq], acc[:, tq:]).T.astype(BF16)

    return begin, walk, finish


def _attn_kernel(dq_ref, dk_ref, dv_ref, lam_ref, subln_ref, sq_ref, sk_ref, sv_ref,
                 od_ref, os_ref, dbuf0, dbuf1, sbuf00, sbuf01, sbuf10, sbuf11, *, lam_init):
    ratio = DIFF_TQ // SB_TQ
    sbufs = ((sbuf00, sbuf01), (sbuf10, sbuf11))
    assert ratio == len(sbufs)

    def q_group(g, carry):
        d_rows = pl.ds(pl.multiple_of(g * DIFF_TQ, DIFF_TQ), DIFF_TQ)
        s_rows = [pl.ds(pl.multiple_of((ratio * g + i) * SB_TQ, SB_TQ), SB_TQ)
                  for i in range(ratio)]
        tiles = [_diff_tile(g, dq_ref[d_rows, :], dk_ref, dv_ref, lam_ref, subln_ref,
                            dbuf0, dbuf1, lam_init)]
        tiles += [_sb_tile(ratio * g + i, sq_ref[s_rows[i], :], sk_ref, sv_ref, *sbufs[i])
                  for i in range(ratio)]
        order = list(range(1, len(tiles))) + [0]
        states = [None] * len(tiles)
        for phase in range(max(len(t[0]) for t in tiles)):
            for t in order:
                if phase < len(tiles[t][0]):
                    states[t] = tiles[t][0][phase](states[t])
        states = [walk(st) for (_, walk, _), st in zip(tiles, states)]
        outs = [finish(st) for (_, _, finish), st in zip(tiles, states)]
        od_ref[d_rows, :] = outs[0]
        for i in range(ratio):
            os_ref[s_rows[i], :] = outs[1 + i]
        return carry

    lax.fori_loop(0, dq_ref.shape[0] // DIFF_TQ, q_group, 0)


def _attention(proj, lam_p, subln, lam_init):
    b, s, _ = proj.shape
    assert DIFF_HEADS == SB_PAIRS
    col = lambda first: pl.BlockSpec((None, s, LANES), lambda bi, h: (bi, 0, first + h))
    dq, dk, dv = (i * DIFF_WIDTH // LANES for i in range(3))
    sq, sk, sv = (3 * DIFF_WIDTH // LANES + i * SB_PAIRS for i in range(3))
    return pl.pallas_call(
        functools.partial(_attn_kernel, lam_init=lam_init),
        grid=(b, DIFF_HEADS),
        in_specs=[
            col(dq), col(dk), col(dv),
            pl.BlockSpec((4, HEAD_DIM), lambda bi, h: (0, 0)),
            pl.BlockSpec((LANES, 1), lambda bi, h: (0, 0)),
            col(sq), col(sk), col(sv),
        ],
        out_specs=[col(0), col(0)],
        out_shape=[jax.ShapeDtypeStruct((b, s, DIFF_WIDTH), BF16),
                   jax.ShapeDtypeStruct((b, s, SB_WIDTH), BF16)],
        scratch_shapes=([pltpu.VMEM((DIFF_TK, 2 * DIFF_TQ), F32)] * 2
                        + [pltpu.VMEM((SB_TK, 2 * SB_TQ), F32)] * 4),
        compiler_params=pltpu.CompilerParams(
            dimension_semantics=("arbitrary", "arbitrary"), vmem_limit_bytes=ATTN_VMEM_LIMIT),
        name="attention",
    )(proj, proj, proj, lam_p, subln, proj, proj, proj)


def _merge_ffn_kernel(x_ref, od_ref, os_ref, gd_ref, gs_ref, wod_ref, wos_ref, wo_ref,
                      g_ref, wi_ref, wd_ref, gf_ref, o_ref, *, final_norm):
    pd = jnp.dot(od_ref[...], wod_ref[...], preferred_element_type=F32)
    ps = jnp.dot(os_ref[...], wos_ref[...], preferred_element_type=F32)
    merged = gd_ref[...].astype(F32) * pd + gs_ref[...].astype(F32) * ps
    x = x_ref[...] + jnp.dot(merged.astype(BF16), wo_ref[...], preferred_element_type=F32)

    ms = jnp.mean(x * x, axis=-1, keepdims=True)
    h = (x * lax.rsqrt(ms + NORM_EPS) * g_ref[...]).astype(BF16)
    y = x
    for c0 in range(0, FFN_HIDDEN, FFN_CH):
        c1 = min(c0 + FFN_CH, FFN_HIDDEN)
        gate = jnp.dot(h, wi_ref[:, c0:c1], preferred_element_type=F32)
        up = jnp.dot(h, wi_ref[:, FFN_HIDDEN + c0:FFN_HIDDEN + c1],
                     preferred_element_type=F32)
        act = (gate * jax.nn.sigmoid(gate) * up).astype(BF16)
        y = y + jnp.dot(act, wd_ref[c0:c1, :], preferred_element_type=F32)
    if final_norm:
        ms = jnp.mean(y * y, axis=-1, keepdims=True)
        y = y * lax.rsqrt(ms + NORM_EPS) * gf_ref[...]
    o_ref[...] = y


def _merge_ffn(x2d, o_diff, o_sb, proj2d, layer, w_o_diff, w_o_sb, w_out, g, w_in, w_down,
               g_final, final_norm):
    t_rows = x2d.shape[0]
    tm = FFN_TM
    gd_blk = GATE_COL0 // D_MODEL
    once = pl.Buffered(1)
    rows = lambda width, blk=0: pl.BlockSpec((tm, width), lambda i: (i, blk))
    whole = lambda shape: pl.BlockSpec(shape, lambda i: (0, 0), pipeline_mode=once)
    of_layer = lambda shape: pl.BlockSpec((None,) + shape, lambda i: (layer, 0, 0),
                                          pipeline_mode=once)
    return pl.pallas_call(
        functools.partial(_merge_ffn_kernel, final_norm=final_norm),
        grid=(t_rows // tm,),
        in_specs=[
            rows(D_MODEL), rows(DIFF_WIDTH), rows(SB_WIDTH),
            rows(D_MODEL, gd_blk), rows(D_MODEL, gd_blk + 1),
            of_layer((DIFF_WIDTH, D_MODEL)), of_layer((SB_WIDTH, D_MODEL)),
            of_layer((D_MODEL, D_MODEL)), whole((1, D_MODEL)),
            of_layer((D_MODEL, 2 * FFN_HIDDEN)), of_layer((FFN_HIDDEN, D_MODEL)),
            whole((1, D_MODEL)),
        ],
        out_specs=rows(D_MODEL),
        out_shape=jax.ShapeDtypeStruct((t_rows, D_MODEL), F32),
        compiler_params=pltpu.CompilerParams(
            dimension_semantics=("arbitrary",), vmem_limit_bytes=VMEM_LIMIT),
        name="merge_ffn",
    )(x2d, o_diff, o_sb, proj2d, proj2d, w_o_diff, w_o_sb, w_out, g, w_in, w_down, g_final)


def _rope_tables(seq):
    half = HEAD_DIM // 2
    pos = jnp.arange(seq, dtype=F32)
    inv = ROPE_THETA ** (-jnp.arange(0, HEAD_DIM, 2, dtype=F32) / HEAD_DIM)
    ang = pos[:, None] * inv[None, :]
    cos, sin = jnp.cos(ang), jnp.sin(ang)
    cos_t = jnp.tile(cos, (1, LANES // half))
    sin_t = jnp.tile(jnp.concatenate([-sin, sin], axis=1), (1, LANES // HEAD_DIM))
    return cos_t, sin_t


def kernel(x, norm_attn, w_in, b_gate, diff_lambda, diff_subln, w_o_diff, w_o_sb, w_out,
           norm_ffn, w_ffn_in, w_ffn_out, norm_final):
    b, s, d = x.shape
    cos_t, sin_t = _rope_tables(s)
    x2d = x.reshape(b * s, d)
    g_final = norm_final.reshape(1, d)
    w_in, w_o_diff, w_o_sb, w_out, w_ffn_in, w_ffn_out = (
        w.astype(BF16) for w in (w_in, w_o_diff, w_o_sb, w_out, w_ffn_in, w_ffn_out))
    for layer in range(DEPTH):
        lam_init = 0.8 - 0.6 * math.exp(-0.3 * layer)
        proj2d = _in_proj(x2d, norm_attn[layer].reshape(1, d), w_in, layer,
                          b_gate[layer].reshape(1, 2 * d), cos_t, sin_t, s)
        proj = proj2d.reshape(b, s, IN_COLS)
        o_diff, o_sb = _attention(proj, diff_lambda[layer],
                                  diff_subln[layer].reshape(LANES, 1), lam_init)
        x2d = _merge_ffn(
            x2d, o_diff.reshape(b * s, DIFF_WIDTH), o_sb.reshape(b * s, SB_WIDTH), proj2d,
            layer, w_o_diff, w_o_sb, w_out, norm_ffn[layer].reshape(1, d), w_ffn_in,
            w_ffn_out, g_final, layer == DEPTH - 1)
    return x2d.reshape(b, s, d)
```

```python
import functools
import math

import jax
import jax.numpy as jnp
from jax import lax
from jax.experimental import pallas as pl
from jax.experimental.pallas import tpu as pltpu

D_MODEL = 1024
DEPTH = 2
HEAD_DIM = 64
LANES = 128
DIFF_HEADS = 4
DIFF_WIDTH = DIFF_HEADS * 2 * HEAD_DIM
SB_HEADS = 8
SB_WIDTH = SB_HEADS * HEAD_DIM
SB_PAIRS = SB_WIDTH // LANES
IN_COLS = 3 * DIFF_WIDTH + 3 * SB_WIDTH + 2 * D_MODEL
GATE_COL0 = 3 * DIFF_WIDTH + 3 * SB_WIDTH
FFN_HIDDEN = 2816
ROPE_THETA = 10000.0
NORM_EPS = 1e-6
NEG_BIG = -1e30
LOG2E = math.log2(math.e)
SB_DEAD_LOG2 = -160.0

VMEM_LIMIT = 48 * 1024 * 1024
ATTN_VMEM_LIMIT = 56 * 1024 * 1024

IN_TM, IN_CW = 1024, 512
DIFF_TQ, DIFF_TK, DIFF_CC = 1024, 512, 256
SB_TQ, SB_TK, SB_CC = 256, 256, 256
FFN_TM, FFN_CH = 512, 512

BF16 = jnp.bfloat16
F32 = jnp.float32


def _nt_dot(a, b):
    return lax.dot_general(a, b, (((1,), (1,)), ((), ())), preferred_element_type=F32)


def _in_proj_kernel(x_ref, g_ref, w_ref, bg_ref, cos_ref, sin_ref, o_ref):
    x = x_ref[...]
    ms = jnp.mean(x * x, axis=-1, keepdims=True)
    h = (x * lax.rsqrt(ms + NORM_EPS) * g_ref[...]).astype(BF16)

    scale = HEAD_DIM ** -0.5 * LOG2E
    cw = IN_CW
    assert DIFF_WIDTH % cw == 0 and SB_WIDTH % cw == 0
    cos = cos_ref[...]
    sin = sin_ref[...]
    lane = lax.broadcasted_iota(jnp.int32, cos.shape, 1)
    first_half = (lane % HEAD_DIM) < (HEAD_DIM // 2)

    for c0 in range(0, IN_COLS, cw):
        acc = jnp.dot(h, w_ref[:, c0:c0 + cw], preferred_element_type=F32)
        is_q = c0 < DIFF_WIDTH or 3 * DIFF_WIDTH <= c0 < 3 * DIFF_WIDTH + SB_WIDTH
        if c0 < 2 * DIFF_WIDTH:
            for b0 in range(0, cw, LANES):
                t = acc[:, b0:b0 + LANES]
                partner = jnp.where(first_half,
                                    pltpu.roll(t, LANES - HEAD_DIM // 2, 1),
                                    pltpu.roll(t, HEAD_DIM // 2, 1))
                r = t * cos + partner * sin
                if is_q:
                    r = r * scale
                o_ref[:, c0 + b0:c0 + b0 + LANES] = r.astype(BF16)
        elif c0 < GATE_COL0:
            if is_q:
                acc = acc * scale
            o_ref[:, c0:c0 + cw] = acc.astype(BF16)
        else:
            g0 = c0 - GATE_COL0
            o_ref[:, c0:c0 + cw] = jax.nn.sigmoid(acc + bg_ref[:, g0:g0 + cw]).astype(BF16)


def _in_proj(x2d, g, w_all, layer, bg, cos_t, sin_t, seq):
    t_rows = x2d.shape[0]
    pos_tiles = seq // IN_TM
    return pl.pallas_call(
        _in_proj_kernel,
        grid=(t_rows // IN_TM,),
        in_specs=[
            pl.BlockSpec((IN_TM, D_MODEL), lambda i: (i, 0)),
            pl.BlockSpec((1, D_MODEL), lambda i: (0, 0)),
            pl.BlockSpec((None, D_MODEL, IN_COLS), lambda i: (layer, 0, 0),
                         pipeline_mode=pl.Buffered(1)),
            pl.BlockSpec((1, 2 * D_MODEL), lambda i: (0, 0)),
            pl.BlockSpec((IN_TM, LANES), lambda i: (i % pos_tiles, 0)),
            pl.BlockSpec((IN_TM, LANES), lambda i: (i % pos_tiles, 0)),
        ],
        out_specs=pl.BlockSpec((IN_TM, IN_COLS), lambda i: (i, 0)),
        out_shape=jax.ShapeDtypeStruct((t_rows, IN_COLS), BF16),
        compiler_params=pltpu.CompilerParams(
            dimension_semantics=("arbitrary",), vmem_limit_bytes=VMEM_LIMIT),
        name="in_proj",
    )(x2d, g, w_all, bg, cos_t, sin_t)


def _tn_dot(a, b):
    return lax.dot_general(a, b, (((0,), (0,)), ((), ())), preferred_element_type=F32)


def _diff_tile(qi, q, k_ref, v_ref, lam_ref, subln_ref, buf0, buf1, lam_init):
    tq, tk, cc = DIFF_TQ, DIFF_TK, DIFF_CC
    lane = lax.broadcasted_iota(jnp.int32, q.shape, 1)
    zero = jnp.zeros_like(q)
    qs = jnp.concatenate([jnp.where(lane < HEAD_DIM, q, zero),
                          jnp.where(lane >= HEAD_DIM, q, zero)], axis=0)
    chunks = [slice(c * cc, (c + 1) * cc) for c in range(2 * tq // cc)]
    key = lax.broadcasted_iota(jnp.int32, (tk, cc), 0)
    qry = lax.broadcasted_iota(jnp.int32, (tk, cc), 1)

    def scores(cols, k, diag):
        q0 = cols.start % tq
        if diag is not None and q0 + cc <= diag:
            return None
        s = _nt_dot(k, qs[cols])
        if diag is not None and q0 < diag + tk - 1:
            s = jnp.where((key + diag) <= (qry + q0), s, NEG_BIG)
        return s, jnp.max(s, axis=0, keepdims=True)

    def accumulate(s, tile_max, m, l, acc, v):
        m_new = jnp.maximum(m, tile_max)
        alpha = jnp.exp2(m - m_new)
        p = jnp.exp2(s - m_new)
        l = alpha * l + jnp.sum(p, axis=0, keepdims=True)
        acc = alpha * acc + _tn_dot(v, p.astype(BF16))
        return m_new, l, acc

    def step(kb_next, kb_pending, diag, buf_in, buf_out, carry):
        m, l, acc, tile_max = carry
        k = k_ref[pl.ds(pl.multiple_of(kb_next * tk, tk), tk), :]
        v = v_ref[pl.ds(pl.multiple_of(kb_pending * tk, tk), tk), :]
        m_n, l_n, acc_n, tile_max_n = [], [], [], []
        for c, cols in enumerate(chunks):
            scored = scores(cols, k, diag)
            if scored is not None:
                buf_out[:, cols] = scored[0]
            state = (m[c], l[c], acc[c])
            if tile_max[c] is not None:
                state = accumulate(buf_in[:, cols], tile_max[c], *state, v)
            m_n.append(state[0])
            l_n.append(state[1])
            acc_n.append(state[2])
            tile_max_n.append(None if scored is None else scored[1])
        return m_n, l_n, acc_n, tile_max_n

    per_q = tq // tk
    assert per_q == 2
    d0, d1 = qi * per_q, qi * per_q + 1

    def begin_scores(_):
        kd1 = k_ref[pl.ds(pl.multiple_of(d1 * tk, tk), tk), :]
        tile_max = []
        for cols in chunks:
            scored = scores(cols, kd1, tk)
            if scored is not None:
                buf1[:, cols] = scored[0]
            tile_max.append(None if scored is None else scored[1])
        n_c = len(chunks)
        return ([jnp.full((1, cc), NEG_BIG, F32)] * n_c, [jnp.zeros((1, cc), F32)] * n_c,
                [jnp.zeros((LANES, cc), F32)] * n_c, tile_max)

    begin = (begin_scores, lambda carry: step(d0, d1, 0, buf1, buf0, carry))

    def pair(j, carry):
        kb = d0 - 1 - 2 * j
        carry = step(kb, kb + 1, None, buf0, buf1, carry)
        return step(kb - 1, kb, None, buf1, buf0, carry)

    def walk(carry):
        return lax.fori_loop(0, qi, pair, carry)

    def finish(carry):
        m, l, acc, tile_max = carry
        v_last = v_ref[pl.ds(0, tk), :]
        outs = [accumulate(buf0[:, cols], tile_max[c], m[c], l[c], acc[c], v_last)
                for c, cols in enumerate(chunks)]
        o = jnp.concatenate([oc[2] * (1.0 / oc[1]) for oc in outs], axis=1)

        lp = lam_ref[...]
        e1 = jnp.exp(jnp.sum(lp[0:1, :] * lp[1:2, :], axis=1, keepdims=True))
        e2 = jnp.exp(jnp.sum(lp[2:3, :] * lp[3:4, :], axis=1, keepdims=True))
        lam = e1 - e2 + lam_init
        o = o[:, :tq] - lam * o[:, tq:]
        ms = jnp.mean(o * o, axis=0, keepdims=True)
        o = o * lax.rsqrt(ms + NORM_EPS) * subln_ref[...] * (1.0 - lam_init)
        return o.T.astype(BF16)

    return begin, walk, finish


def _sb_tile(qi, q, k_ref, v_ref, buf0, buf1):
    tq, tk, cc = SB_TQ, SB_TK, SB_CC
    lane = lax.broadcasted_iota(jnp.int32, q.shape, 1)
    zero = jnp.zeros_like(q)
    per_q = tq // tk
    qs = jnp.concatenate([jnp.where(lane < HEAD_DIM, q, zero),
                          jnp.where(lane >= HEAD_DIM, q, zero)], axis=0)
    ss = lax.broadcasted_iota(jnp.int32, (tk, tk), 0)
    jj = lax.broadcasted_iota(jnp.int32, (tk, tk), 1)
    neg_suffix = jnp.where(jj > ss, -1.0, 0.0).astype(BF16)
    chunks = [slice(c * cc, (c + 1) * cc) for c in range(2 * tq // cc)]
    key = lax.broadcasted_iota(jnp.int32, (tk, cc), 0)
    qry = lax.broadcasted_iota(jnp.int32, (tk, cc), 1)

    def scores(cols, k, diag):
        q0 = cols.start % tq
        if diag is not None and q0 + cc <= diag + 1:
            return None, jnp.zeros((1, cc), F32)
        if diag is not None and q0 >= diag + tk:
            diag = None
        z = _nt_dot(k, qs[cols])
        sp = jnp.maximum(z, 0.0) + jnp.log2(1.0 + jnp.exp2(-jnp.abs(z)))
        if diag is not None:
            mask = (key + diag) < (qry + q0)
            sp = jnp.where(mask, sp, 0.0)
        arg = (z - sp) + jnp.dot(neg_suffix, sp.astype(BF16), preferred_element_type=F32)
        if diag is not None:
            arg = jnp.where(mask, arg, NEG_BIG)
        return arg, jnp.sum(sp, axis=0, keepdims=True)

    def accumulate(arg, rem, acc, v):
        a = jnp.exp2(arg + rem)
        return acc + _tn_dot(v, a.astype(BF16))

    def step(kb_next, kb_pending, diag, buf_in, buf_out, carry, pending_empty=(),
             exists=None):
        rem, acc, key_sum = carry
        k = k_ref[pl.ds(pl.multiple_of(kb_next * tk, tk), tk), :]
        v = v_ref[pl.ds(pl.multiple_of(kb_pending * tk, tk), tk), :]
        if exists is not None:
            exists_row = jnp.full((1, cc), exists, jnp.int32) > 0
        rem_n, acc_n, key_sum_n = [], [], []
        for c, cols in enumerate(chunks):
            arg_next, ks = scores(cols, k, diag)
            buf_out[:, cols] = arg_next
            acc_n.append(acc[c] if c in pending_empty
                         else accumulate(buf_in[:, cols], rem[c], acc[c], v))
            rem_next = rem[c] - key_sum[c]
            if exists is not None:
                rem_next = jnp.where(exists_row, rem_next, NEG_BIG)
                ks = jnp.where(exists_row, ks, 0.0)
            rem_n.append(rem_next)
            key_sum_n.append(ks)
        return rem_n, acc_n, key_sum_n

    def next_live(carry):
        rem, _, key_sum = carry
        top = functools.reduce(jnp.maximum, [r - ks for r, ks in zip(rem, key_sum)])
        return (jnp.max(top) > SB_DEAD_LOG2).astype(jnp.int32)

    assert per_q in (1, 2)
    d0 = qi * per_q
    d_first = d0 + per_q - 1
    f1 = jnp.maximum(d0 - 1, 0)

    empty = []
    first_buf = buf0 if per_q == 2 else buf1

    def begin_scores(_):
        k1 = k_ref[pl.ds(pl.multiple_of(d_first * tk, tk), tk), :]
        key_sum = []
        for c, cols in enumerate(chunks):
            arg, ks = scores(cols, k1, (per_q - 1) * tk)
            if arg is None:
                empty.append(c)
            else:
                first_buf[:, cols] = arg
            key_sum.append(ks)
        return ([jnp.zeros((1, cc), F32)] * len(chunks),
                [jnp.zeros((LANES, cc), F32)] * len(chunks), key_sum)

    to_f1 = lambda carry: step(f1, d0, None, buf1, buf0, carry, exists=jnp.minimum(qi, 1))
    if per_q == 2:
        begin = (begin_scores,
                 lambda carry: step(d0, d_first, 0, buf0, buf1, carry, pending_empty=empty),
                 to_f1)
    else:
        begin = (begin_scores, to_f1)

    def pair(state):
        j, _, carry = state
        kb = f1 - 1 - 2 * j
        carry = step(kb, kb + 1, None, buf0, buf1, carry)
        carry = step(jnp.maximum(kb - 1, 0), kb, None, buf1, buf0, carry,
                     exists=jnp.clip(kb, 0, 1))
        return j + 1, next_live(carry), carry

    def walk(carry):
        max_pairs = (f1 + 1) // 2
        n_pairs, _, carry = lax.while_loop(
            lambda st: (st[0] < max_pairs) & (st[1] > 0), pair,
            (jnp.int32(0), next_live(carry), carry))
        return n_pairs, carry

    def finish(state):
        n_pairs, (rem, acc, _) = state
        pending = jnp.maximum(f1 - 2 * n_pairs, 0)
        v = v_ref[pl.ds(pl.multiple_of(pending * tk, tk), tk), :]
        acc = jnp.concatenate([accumulate(buf0[:, cols], rem[c], acc[c], v)
                               for c, cols in enumerate(chunks)], axis=1)
        feat = lax.broadcasted_iota(jnp.int32, (LANES, tq), 0)
        return jnp.where(feat < HEAD_DIM, acc[:, :tq], acc[:, tq:]).T.astype(BF16)

    return begin, walk, finish


def _attn_kernel(dq_ref, dk_ref, dv_ref, lam_ref, subln_ref, sq_ref, sk_ref, sv_ref,
                 od_ref, os_ref, dbuf0, dbuf1, *sb_bufs, lam_init):
    ratio = DIFF_TQ // SB_TQ
    sbufs = [sb_bufs[2 * i:2 * i + 2] for i in range(ratio)]
    assert len(sb_bufs) == 2 * ratio

    def q_group(g, carry):
        d_rows = pl.ds(pl.multiple_of(g * DIFF_TQ, DIFF_TQ), DIFF_TQ)
        s_rows = [pl.ds(pl.multiple_of((ratio * g + i) * SB_TQ, SB_TQ), SB_TQ)
                  for i in range(ratio)]
        tiles = [_diff_tile(g, dq_ref[d_rows, :], dk_ref, dv_ref, lam_ref, subln_ref,
                            dbuf0, dbuf1, lam_init)]
        tiles += [_sb_tile(ratio * g + i, sq_ref[s_rows[i], :], sk_ref, sv_ref, *sbufs[i])
                  for i in range(ratio)]
        order = list(range(1, len(tiles))) + [0]
        states = [None] * len(tiles)
        for phase in range(max(len(t[0]) for t in tiles)):
            for t in order:
                if phase < len(tiles[t][0]):
                    states[t] = tiles[t][0][phase](states[t])
        states = [walk(st) for (_, walk, _), st in zip(tiles, states)]
        outs = [finish(st) for (_, _, finish), st in zip(tiles, states)]
        od_ref[d_rows, :] = outs[0]
        for i in range(ratio):
            os_ref[s_rows[i], :] = outs[1 + i]
        return carry

    lax.fori_loop(0, dq_ref.shape[0] // DIFF_TQ, q_group, 0)


def _attention(proj, lam_p, subln, lam_init):
    b, s, _ = proj.shape
    assert DIFF_HEADS == SB_PAIRS
    col = lambda first: pl.BlockSpec((None, s, LANES), lambda bi, h: (bi, 0, first + h))
    dq, dk, dv = (i * DIFF_WIDTH // LANES for i in range(3))
    sq, sk, sv = (3 * DIFF_WIDTH // LANES + i * SB_PAIRS for i in range(3))
    return pl.pallas_call(
        functools.partial(_attn_kernel, lam_init=lam_init),
        grid=(b, DIFF_HEADS),
        in_specs=[
            col(dq), col(dk), col(dv),
            pl.BlockSpec((4, HEAD_DIM), lambda bi, h: (0, 0)),
            pl.BlockSpec((LANES, 1), lambda bi, h: (0, 0)),
            col(sq), col(sk), col(sv),
        ],
        out_specs=[col(0), col(0)],
        out_shape=[jax.ShapeDtypeStruct((b, s, DIFF_WIDTH), BF16),
                   jax.ShapeDtypeStruct((b, s, SB_WIDTH), BF16)],
        scratch_shapes=([pltpu.VMEM((DIFF_TK, 2 * DIFF_TQ), F32)] * 2
                        + [pltpu.VMEM((SB_TK, 2 * SB_TQ), F32)] * (2 * DIFF_TQ // SB_TQ)),
        compiler_params=pltpu.CompilerParams(
            dimension_semantics=("arbitrary", "arbitrary"), vmem_limit_bytes=ATTN_VMEM_LIMIT),
        name="attention",
    )(proj, proj, proj, lam_p, subln, proj, proj, proj)


def _merge_ffn_kernel(x_ref, od_ref, os_ref, gd_ref, gs_ref, wod_ref, wos_ref, wo_ref,
                      g_ref, wi_ref, wd_ref, gf_ref, o_ref, *, final_norm):
    pd = jnp.dot(od_ref[...], wod_ref[...], preferred_element_type=F32)
    ps = jnp.dot(os_ref[...], wos_ref[...], preferred_element_type=F32)
    merged = gd_ref[...].astype(F32) * pd + gs_ref[...].astype(F32) * ps
    x = x_ref[...] + jnp.dot(merged.astype(BF16), wo_ref[...], preferred_element_type=F32)

    ms = jnp.mean(x * x, axis=-1, keepdims=True)
    h = (x * lax.rsqrt(ms + NORM_EPS) * g_ref[...]).astype(BF16)
    y = x
    for c0 in range(0, FFN_HIDDEN, FFN_CH):
        c1 = min(c0 + FFN_CH, FFN_HIDDEN)
        gate = jnp.dot(h, wi_ref[:, c0:c1], preferred_element_type=F32)
        up = jnp.dot(h, wi_ref[:, FFN_HIDDEN + c0:FFN_HIDDEN + c1],
                     preferred_element_type=F32)
        act = (gate * jax.nn.sigmoid(gate) * up).astype(BF16)
        y = y + jnp.dot(act, wd_ref[c0:c1, :], preferred_element_type=F32)
    if final_norm:
        ms = jnp.mean(y * y, axis=-1, keepdims=True)
        y = y * lax.rsqrt(ms + NORM_EPS) * gf_ref[...]
    o_ref[...] = y


def _merge_ffn(x2d, o_diff, o_sb, proj2d, layer, w_o_diff, w_o_sb, w_out, g, w_in, w_down,
               g_final, final_norm):
    t_rows = x2d.shape[0]
    tm = FFN_TM
    gd_blk = GATE_COL0 // D_MODEL
    once = pl.Buffered(1)
    rows = lambda width, blk=0: pl.BlockSpec((tm, width), lambda i: (i, blk))
    whole = lambda shape: pl.BlockSpec(shape, lambda i: (0, 0), pipeline_mode=once)
    of_layer = lambda shape: pl.BlockSpec((None,) + shape, lambda i: (layer, 0, 0),
                                          pipeline_mode=once)
    return pl.pallas_call(
        functools.partial(_merge_ffn_kernel, final_norm=final_norm),
        grid=(t_rows // tm,),
        in_specs=[
            rows(D_MODEL), rows(DIFF_WIDTH), rows(SB_WIDTH),
            rows(D_MODEL, gd_blk), rows(D_MODEL, gd_blk + 1),
            of_layer((DIFF_WIDTH, D_MODEL)), of_layer((SB_WIDTH, D_MODEL)),
            of_layer((D_MODEL, D_MODEL)), whole((1, D_MODEL)),
            of_layer((D_MODEL, 2 * FFN_HIDDEN)), of_layer((FFN_HIDDEN, D_MODEL)),
            whole((1, D_MODEL)),
        ],
        out_specs=rows(D_MODEL),
        out_shape=jax.ShapeDtypeStruct((t_rows, D_MODEL), F32),
        compiler_params=pltpu.CompilerParams(
            dimension_semantics=("arbitrary",), vmem_limit_bytes=VMEM_LIMIT),
        name="merge_ffn",
    )(x2d, o_diff, o_sb, proj2d, proj2d, w_o_diff, w_o_sb, w_out, g, w_in, w_down, g_final)


def _rope_tables(seq):
    half = HEAD_DIM // 2
    pos = jnp.arange(seq, dtype=F32)
    inv = ROPE_THETA ** (-jnp.arange(0, HEAD_DIM, 2, dtype=F32) / HEAD_DIM)
    ang = pos[:, None] * inv[None, :]
    cos, sin = jnp.cos(ang), jnp.sin(ang)
    cos_t = jnp.tile(cos, (1, LANES // half))
    sin_t = jnp.tile(jnp.concatenate([-sin, sin], axis=1), (1, LANES // HEAD_DIM))
    return cos_t, sin_t


def kernel(x, norm_attn, w_in, b_gate, diff_lambda, diff_subln, w_o_diff, w_o_sb, w_out,
           norm_ffn, w_ffn_in, w_ffn_out, norm_final):
    b, s, d = x.shape
    cos_t, sin_t = _rope_tables(s)
    x2d = x.reshape(b * s, d)
    g_final = norm_final.reshape(1, d)
    w_in, w_o_diff, w_o_sb, w_out, w_ffn_in, w_ffn_out = (
        w.astype(BF16) for w in (w_in, w_o_diff, w_o_sb, w_out, w_ffn_in, w_ffn_out))
    for layer in range(DEPTH):
        lam_init = 0.8 - 0.6 * math.exp(-0.3 * layer)
        proj2d = _in_proj(x2d, norm_attn[layer].reshape(1, d), w_in, layer,
                          b_gate[layer].reshape(1, 2 * d), cos_t, sin_t, s)
        proj = proj2d.reshape(b, s, IN_COLS)
        o_diff, o_sb = _attention(proj, diff_lambda[layer],
                                  diff_subln[layer].reshape(LANES, 1), lam_init)
        x2d = _merge_ffn(
            x2d, o_diff.reshape(b * s, DIFF_WIDTH), o_sb.reshape(b * s, SB_WIDTH), proj2d,
            layer, w_o_diff, w_o_sb, w_out, norm_ffn[layer].reshape(1, d), w_ffn_in,
            w_ffn_out, g_final, layer == DEPTH - 1)
    return x2d.reshape(b, s, d)
```

```python
import functools
import math

import jax
import jax.numpy as jnp
from jax import lax
from jax.experimental import pallas as pl
from jax.experimental.pallas import tpu as pltpu

D_MODEL = 1024
DEPTH = 2
HEAD_DIM = 64
LANES = 128
DIFF_HEADS = 4
DIFF_WIDTH = DIFF_HEADS * 2 * HEAD_DIM
SB_HEADS = 8
SB_WIDTH = SB_HEADS * HEAD_DIM
SB_PAIRS = SB_WIDTH // LANES
IN_COLS = 3 * DIFF_WIDTH + 3 * SB_WIDTH + 2 * D_MODEL
GATE_COL0 = 3 * DIFF_WIDTH + 3 * SB_WIDTH
FFN_HIDDEN = 2816
ROPE_THETA = 10000.0
NORM_EPS = 1e-6
NEG_BIG = -1e30
LOG2E = math.log2(math.e)
SB_DEAD_LOG2 = -160.0

VMEM_LIMIT = 48 * 1024 * 1024
ATTN_VMEM_LIMIT = 56 * 1024 * 1024

IN_TM, IN_CW = 1024, 512
DIFF_TQ, DIFF_TK, DIFF_CC = 1024, 512, 256
SB_TQ, SB_TK, SB_CC = 512, 256, 256
FFN_TM, FFN_CH = 512, 512

BF16 = jnp.bfloat16
F32 = jnp.float32


def _nt_dot(a, b):
    return lax.dot_general(a, b, (((1,), (1,)), ((), ())), preferred_element_type=F32)


def _in_proj_kernel(x_ref, g_ref, w_ref, bg_ref, cos_ref, sin_ref, o_ref):
    x = x_ref[...]
    ms = jnp.mean(x * x, axis=-1, keepdims=True)
    h = (x * lax.rsqrt(ms + NORM_EPS) * g_ref[...]).astype(BF16)

    scale = HEAD_DIM ** -0.5 * LOG2E
    cw = IN_CW
    assert DIFF_WIDTH % cw == 0 and SB_WIDTH % cw == 0
    cos = cos_ref[...]
    sin = sin_ref[...]
    lane = lax.broadcasted_iota(jnp.int32, cos.shape, 1)
    first_half = (lane % HEAD_DIM) < (HEAD_DIM // 2)

    for c0 in range(0, IN_COLS, cw):
        acc = jnp.dot(h, w_ref[:, c0:c0 + cw], preferred_element_type=F32)
        is_q = c0 < DIFF_WIDTH or 3 * DIFF_WIDTH <= c0 < 3 * DIFF_WIDTH + SB_WIDTH
        if c0 < 2 * DIFF_WIDTH:
            for b0 in range(0, cw, LANES):
                t = acc[:, b0:b0 + LANES]
                partner = jnp.where(first_half,
                                    pltpu.roll(t, LANES - HEAD_DIM // 2, 1),
                                    pltpu.roll(t, HEAD_DIM // 2, 1))
                r = t * cos + partner * sin
                if is_q:
                    r = r * scale
                o_ref[:, c0 + b0:c0 + b0 + LANES] = r.astype(BF16)
        elif c0 < GATE_COL0:
            if is_q:
                acc = acc * scale
            o_ref[:, c0:c0 + cw] = acc.astype(BF16)
        else:
            g0 = c0 - GATE_COL0
            o_ref[:, c0:c0 + cw] = jax.nn.sigmoid(acc + bg_ref[:, g0:g0 + cw]).astype(BF16)


def _in_proj(x2d, g, w_all, layer, bg, cos_t, sin_t, seq):
    t_rows = x2d.shape[0]
    pos_tiles = seq // IN_TM
    return pl.pallas_call(
        _in_proj_kernel,
        grid=(t_rows // IN_TM,),
        in_specs=[
            pl.BlockSpec((IN_TM, D_MODEL), lambda i: (i, 0)),
            pl.BlockSpec((1, D_MODEL), lambda i: (0, 0)),
            pl.BlockSpec((None, D_MODEL, IN_COLS), lambda i: (layer, 0, 0),
                         pipeline_mode=pl.Buffered(1)),
            pl.BlockSpec((1, 2 * D_MODEL), lambda i: (0, 0)),
            pl.BlockSpec((IN_TM, LANES), lambda i: (i % pos_tiles, 0)),
            pl.BlockSpec((IN_TM, LANES), lambda i: (i % pos_tiles, 0)),
        ],
        out_specs=pl.BlockSpec((IN_TM, IN_COLS), lambda i: (i, 0)),
        out_shape=jax.ShapeDtypeStruct((t_rows, IN_COLS), BF16),
        compiler_params=pltpu.CompilerParams(
            dimension_semantics=("arbitrary",), vmem_limit_bytes=VMEM_LIMIT),
        name="in_proj",
    )(x2d, g, w_all, bg, cos_t, sin_t)


def _tn_dot(a, b):
    return lax.dot_general(a, b, (((0,), (0,)), ((), ())), preferred_element_type=F32)


def _diff_tile(qi, q, k_ref, v_ref, lam_ref, subln_ref, buf0, buf1, lam_init):
    tq, tk, cc = DIFF_TQ, DIFF_TK, DIFF_CC
    lane = lax.broadcasted_iota(jnp.int32, q.shape, 1)
    zero = jnp.zeros_like(q)
    qs = jnp.concatenate([jnp.where(lane < HEAD_DIM, q, zero),
                          jnp.where(lane >= HEAD_DIM, q, zero)], axis=0)
    chunks = [slice(c * cc, (c + 1) * cc) for c in range(2 * tq // cc)]
    key = lax.broadcasted_iota(jnp.int32, (tk, cc), 0)
    qry = lax.broadcasted_iota(jnp.int32, (tk, cc), 1)

    def scores(cols, k, diag):
        q0 = cols.start % tq
        if diag is not None and q0 + cc <= diag:
            return None
        s = _nt_dot(k, qs[cols])
        if diag is not None and q0 < diag + tk - 1:
            s = jnp.where((key + diag) <= (qry + q0), s, NEG_BIG)
        return s, jnp.max(s, axis=0, keepdims=True)

    def accumulate(s, tile_max, m, l, acc, v):
        m_new = jnp.maximum(m, tile_max)
        alpha = jnp.exp2(m - m_new)
        p = jnp.exp2(s - m_new)
        l = alpha * l + jnp.sum(p, axis=0, keepdims=True)
        acc = alpha * acc + _tn_dot(v, p.astype(BF16))
        return m_new, l, acc

    def step(kb_next, kb_pending, diag, buf_in, buf_out, carry):
        m, l, acc, tile_max = carry
        k = k_ref[pl.ds(pl.multiple_of(kb_next * tk, tk), tk), :]
        v = v_ref[pl.ds(pl.multiple_of(kb_pending * tk, tk), tk), :]
        m_n, l_n, acc_n, tile_max_n = [], [], [], []
        for c, cols in enumerate(chunks):
            scored = scores(cols, k, diag)
            if scored is not None:
                buf_out[:, cols] = scored[0]
            state = (m[c], l[c], acc[c])
            if tile_max[c] is not None:
                state = accumulate(buf_in[:, cols], tile_max[c], *state, v)
            m_n.append(state[0])
            l_n.append(state[1])
            acc_n.append(state[2])
            tile_max_n.append(None if scored is None else scored[1])
        return m_n, l_n, acc_n, tile_max_n

    per_q = tq // tk
    assert per_q == 2
    d0, d1 = qi * per_q, qi * per_q + 1

    def begin_scores(_):
        kd1 = k_ref[pl.ds(pl.multiple_of(d1 * tk, tk), tk), :]
        tile_max = []
        for cols in chunks:
            scored = scores(cols, kd1, tk)
            if scored is not None:
                buf1[:, cols] = scored[0]
            tile_max.append(None if scored is None else scored[1])
        n_c = len(chunks)
        return ([jnp.full((1, cc), NEG_BIG, F32)] * n_c, [jnp.zeros((1, cc), F32)] * n_c,
                [jnp.zeros((LANES, cc), F32)] * n_c, tile_max)

    begin = (begin_scores, lambda carry: step(d0, d1, 0, buf1, buf0, carry))

    def pair(j, carry):
        kb = d0 - 1 - 2 * j
        carry = step(kb, kb + 1, None, buf0, buf1, carry)
        return step(kb - 1, kb, None, buf1, buf0, carry)

    def walk(carry):
        return lax.fori_loop(0, qi, pair, carry)

    def finish(carry):
        m, l, acc, tile_max = carry
        v_last = v_ref[pl.ds(0, tk), :]
        outs = [accumulate(buf0[:, cols], tile_max[c], m[c], l[c], acc[c], v_last)
                for c, cols in enumerate(chunks)]
        o = jnp.concatenate([oc[2] * (1.0 / oc[1]) for oc in outs], axis=1)

        lp = lam_ref[...]
        e1 = jnp.exp(jnp.sum(lp[0:1, :] * lp[1:2, :], axis=1, keepdims=True))
        e2 = jnp.exp(jnp.sum(lp[2:3, :] * lp[3:4, :], axis=1, keepdims=True))
        lam = e1 - e2 + lam_init
        o = o[:, :tq] - lam * o[:, tq:]
        ms = jnp.mean(o * o, axis=0, keepdims=True)
        o = o * lax.rsqrt(ms + NORM_EPS) * subln_ref[...] * (1.0 - lam_init)
        return o.T.astype(BF16)

    return begin, walk, finish


def _sb_tile(qi, q, k_ref, v_ref, buf0, buf1):
    tq, tk, cc = SB_TQ, SB_TK, SB_CC
    lane = lax.broadcasted_iota(jnp.int32, q.shape, 1)
    zero = jnp.zeros_like(q)
    per_q = tq // tk
    qs = jnp.concatenate([jnp.where(lane < HEAD_DIM, q, zero),
                          jnp.where(lane >= HEAD_DIM, q, zero)], axis=0)
    ss = lax.broadcasted_iota(jnp.int32, (tk, tk), 0)
    jj = lax.broadcasted_iota(jnp.int32, (tk, tk), 1)
    neg_suffix = jnp.where(jj > ss, -1.0, 0.0).astype(BF16)
    chunks = [slice(c * cc, (c + 1) * cc) for c in range(2 * tq // cc)]
    key = lax.broadcasted_iota(jnp.int32, (tk, cc), 0)
    qry = lax.broadcasted_iota(jnp.int32, (tk, cc), 1)

    def scores(cols, k, diag):
        q0 = cols.start % tq
        if diag is not None and q0 + cc <= diag + 1:
            return None, jnp.zeros((1, cc), F32)
        if diag is not None and q0 >= diag + tk:
            diag = None
        z = _nt_dot(k, qs[cols])
        sp = jnp.maximum(z, 0.0) + jnp.log2(1.0 + jnp.exp2(-jnp.abs(z)))
        if diag is not None:
            mask = (key + diag) < (qry + q0)
            sp = jnp.where(mask, sp, 0.0)
        arg = (z - sp) + jnp.dot(neg_suffix, sp.astype(BF16), preferred_element_type=F32)
        if diag is not None:
            arg = jnp.where(mask, arg, NEG_BIG)
        return arg, jnp.sum(sp, axis=0, keepdims=True)

    def accumulate(arg, rem, acc, v):
        a = jnp.exp2(arg + rem)
        return acc + _tn_dot(v, a.astype(BF16))

    def step(kb_next, kb_pending, diag, buf_in, buf_out, carry, pending_empty=(),
             exists=None):
        rem, acc, key_sum = carry
        k = k_ref[pl.ds(pl.multiple_of(kb_next * tk, tk), tk), :]
        v = v_ref[pl.ds(pl.multiple_of(kb_pending * tk, tk), tk), :]
        if exists is not None:
            exists_row = jnp.full((1, cc), exists, jnp.int32) > 0
        rem_n, acc_n, key_sum_n = [], [], []
        for c, cols in enumerate(chunks):
            arg_next, ks = scores(cols, k, diag)
            buf_out[:, cols] = arg_next
            acc_n.append(acc[c] if c in pending_empty
                         else accumulate(buf_in[:, cols], rem[c], acc[c], v))
            rem_next = rem[c] - key_sum[c]
            if exists is not None:
                rem_next = jnp.where(exists_row, rem_next, NEG_BIG)
                ks = jnp.where(exists_row, ks, 0.0)
            rem_n.append(rem_next)
            key_sum_n.append(ks)
        return rem_n, acc_n, key_sum_n

    def next_live(carry):
        rem, _, key_sum = carry
        top = functools.reduce(jnp.maximum, [r - ks for r, ks in zip(rem, key_sum)])
        return (jnp.max(top) > SB_DEAD_LOG2).astype(jnp.int32)

    assert per_q == 2
    d1, d0 = qi * per_q + 1, qi * per_q
    f1 = jnp.maximum(d0 - 1, 0)

    empty = []

    def begin_scores(_):
        k1 = k_ref[pl.ds(pl.multiple_of(d1 * tk, tk), tk), :]
        key_sum = []
        for c, cols in enumerate(chunks):
            arg, ks = scores(cols, k1, tk)
            if arg is None:
                empty.append(c)
            else:
                buf0[:, cols] = arg
            key_sum.append(ks)
        return ([jnp.zeros((1, cc), F32)] * len(chunks),
                [jnp.zeros((LANES, cc), F32)] * len(chunks), key_sum)

    begin = (begin_scores,
             lambda carry: step(d0, d1, 0, buf0, buf1, carry, pending_empty=empty),
             lambda carry: step(f1, d0, None, buf1, buf0, carry, exists=jnp.minimum(qi, 1)))

    def pair(state):
        j, _, carry = state
        kb = f1 - 1 - 2 * j
        carry = step(kb, kb + 1, None, buf0, buf1, carry)
        carry = step(jnp.maximum(kb - 1, 0), kb, None, buf1, buf0, carry,
                     exists=jnp.clip(kb, 0, 1))
        return j + 1, next_live(carry), carry

    def walk(carry):
        n_pairs, _, carry = lax.while_loop(
            lambda st: (st[0] < qi) & (st[1] > 0), pair,
            (jnp.int32(0), next_live(carry), carry))
        return n_pairs, carry

    def finish(state):
        n_pairs, (rem, acc, _) = state
        pending = jnp.maximum(f1 - 2 * n_pairs, 0)
        v = v_ref[pl.ds(pl.multiple_of(pending * tk, tk), tk), :]
        acc = jnp.concatenate([accumulate(buf0[:, cols], rem[c], acc[c], v)
                               for c, cols in enumerate(chunks)], axis=1)
        feat = lax.broadcasted_iota(jnp.int32, (LANES, tq), 0)
        return jnp.where(feat < HEAD_DIM, acc[:, :tq], acc[:, tq:]).T.astype(BF16)

    return begin, walk, finish


def _attn_kernel(dq_ref, dk_ref, dv_ref, lam_ref, subln_ref, sq_ref, sk_ref, sv_ref,
                 od_ref, os_ref, dbuf0, dbuf1, sbuf00, sbuf01, sbuf10, sbuf11, *, lam_init):
    ratio = DIFF_TQ // SB_TQ
    sbufs = ((sbuf00, sbuf01), (sbuf10, sbuf11))
    assert ratio == len(sbufs)

    def q_group(g, carry):
        d_rows = pl.ds(pl.multiple_of(g * DIFF_TQ, DIFF_TQ), DIFF_TQ)
        s_rows = [pl.ds(pl.multiple_of((ratio * g + i) * SB_TQ, SB_TQ), SB_TQ)
                  for i in range(ratio)]
        tiles = [_diff_tile(g, dq_ref[d_rows, :], dk_ref, dv_ref, lam_ref, subln_ref,
                            dbuf0, dbuf1, lam_init)]
        tiles += [_sb_tile(ratio * g + i, sq_ref[s_rows[i], :], sk_ref, sv_ref, *sbufs[i])
                  for i in range(ratio)]
        order = list(range(1, len(tiles))) + [0]
        states = [None] * len(tiles)
        for phase in range(max(len(t[0]) for t in tiles)):
            for t in order:
                if phase < len(tiles[t][0]):
                    states[t] = tiles[t][0][phase](states[t])
        states = [walk(st) for (_, walk, _), st in zip(tiles, states)]
        outs = [finish(st) for (_, _, finish), st in zip(tiles, states)]
        od_ref[d_rows, :] = outs[0]
        for i in range(ratio):
            os_ref[s_rows[i], :] = outs[1 + i]
        return carry

    lax.fori_loop(0, dq_ref.shape[0] // DIFF_TQ, q_group, 0)


def _attention(proj, lam_p, subln, lam_init):
    b, s, _ = proj.shape
    assert DIFF_HEADS == SB_PAIRS
    col = lambda first: pl.BlockSpec((None, s, LANES), lambda bi, h: (bi, 0, first + h))
    dq, dk, dv = (i * DIFF_WIDTH // LANES for i in range(3))
    sq, sk, sv = (3 * DIFF_WIDTH // LANES + i * SB_PAIRS for i in range(3))
    return pl.pallas_call(
        functools.partial(_attn_kernel, lam_init=lam_init),
        grid=(b, DIFF_HEADS),
        in_specs=[
            col(dq), col(dk), col(dv),
            pl.BlockSpec((4, HEAD_DIM), lambda bi, h: (0, 0)),
            pl.BlockSpec((LANES, 1), lambda bi, h: (0, 0)),
            col(sq), col(sk), col(sv),
        ],
        out_specs=[col(0), col(0)],
        out_shape=[jax.ShapeDtypeStruct((b, s, DIFF_WIDTH), BF16),
                   jax.ShapeDtypeStruct((b, s, SB_WIDTH), BF16)],
        scratch_shapes=([pltpu.VMEM((DIFF_TK, 2 * DIFF_TQ), F32)] * 2
                        + [pltpu.VMEM((SB_TK, 2 * SB_TQ), F32)] * 4),
        compiler_params=pltpu.CompilerParams(
            dimension_semantics=("arbitrary", "arbitrary"), vmem_limit_bytes=ATTN_VMEM_LIMIT),
        name="attention",
    )(proj, proj, proj, lam_p, subln, proj, proj, proj)


def _merge_ffn_kernel(x_ref, od_ref, os_ref, gd_ref, gs_ref, wod_ref, wos_ref, wo_ref,
                      g_ref, wi_ref, wd_ref, gf_ref, o_ref, act_ref, *, final_norm):
    pd = jnp.dot(od_ref[...], wod_ref[...], preferred_element_type=F32)
    ps = jnp.dot(os_ref[...], wos_ref[...], preferred_element_type=F32)
    merged = gd_ref[...].astype(F32) * pd + gs_ref[...].astype(F32) * ps
    x = x_ref[...] + jnp.dot(merged.astype(BF16), wo_ref[...], preferred_element_type=F32)

    ms = jnp.mean(x * x, axis=-1, keepdims=True)
    h = (x * lax.rsqrt(ms + NORM_EPS) * g_ref[...]).astype(BF16)
    for c0 in range(0, FFN_HIDDEN, FFN_CH):
        c1 = min(c0 + FFN_CH, FFN_HIDDEN)
        gate = jnp.dot(h, wi_ref[:, c0:c1], preferred_element_type=F32)
        up = jnp.dot(h, wi_ref[:, FFN_HIDDEN + c0:FFN_HIDDEN + c1],
                     preferred_element_type=F32)
        act_ref[:, c0:c1] = (gate * jax.nn.sigmoid(gate) * up).astype(BF16)
    y = x + jnp.dot(act_ref[...], wd_ref[...], preferred_element_type=F32)
    if final_norm:
        ms = jnp.mean(y * y, axis=-1, keepdims=True)
        y = y * lax.rsqrt(ms + NORM_EPS) * gf_ref[...]
    o_ref[...] = y


def _merge_ffn(x2d, o_diff, o_sb, proj2d, layer, w_o_diff, w_o_sb, w_out, g, w_in, w_down,
               g_final, final_norm):
    t_rows = x2d.shape[0]
    tm = FFN_TM
    gd_blk = GATE_COL0 // D_MODEL
    once = pl.Buffered(1)
    rows = lambda width, blk=0: pl.BlockSpec((tm, width), lambda i: (i, blk))
    whole = lambda shape: pl.BlockSpec(shape, lambda i: (0, 0), pipeline_mode=once)
    of_layer = lambda shape: pl.BlockSpec((None,) + shape, lambda i: (layer, 0, 0),
                                          pipeline_mode=once)
    return pl.pallas_call(
        functools.partial(_merge_ffn_kernel, final_norm=final_norm),
        grid=(t_rows // tm,),
        in_specs=[
            rows(D_MODEL), rows(DIFF_WIDTH), rows(SB_WIDTH),
            rows(D_MODEL, gd_blk), rows(D_MODEL, gd_blk + 1),
            of_layer((DIFF_WIDTH, D_MODEL)), of_layer((SB_WIDTH, D_MODEL)),
            of_layer((D_MODEL, D_MODEL)), whole((1, D_MODEL)),
            of_layer((D_MODEL, 2 * FFN_HIDDEN)), of_layer((FFN_HIDDEN, D_MODEL)),
            whole((1, D_MODEL)),
        ],
        out_specs=rows(D_MODEL),
        out_shape=jax.ShapeDtypeStruct((t_rows, D_MODEL), F32),
        scratch_shapes=[pltpu.VMEM((tm, FFN_HIDDEN), BF16)],
        compiler_params=pltpu.CompilerParams(
            dimension_semantics=("arbitrary",), vmem_limit_bytes=VMEM_LIMIT),
        name="merge_ffn",
    )(x2d, o_diff, o_sb, proj2d, proj2d, w_o_diff, w_o_sb, w_out, g, w_in, w_down, g_final)


def _rope_tables(seq):
    half = HEAD_DIM // 2
    pos = jnp.arange(seq, dtype=F32)
    inv = ROPE_THETA ** (-jnp.arange(0, HEAD_DIM, 2, dtype=F32) / HEAD_DIM)
    ang = pos[:, None] * inv[None, :]
    cos, sin = jnp.cos(ang), jnp.sin(ang)
    cos_t = jnp.tile(cos, (1, LANES // half))
    sin_t = jnp.tile(jnp.concatenate([-sin, sin], axis=1), (1, LANES // HEAD_DIM))
    return cos_t, sin_t


def kernel(x, norm_attn, w_in, b_gate, diff_lambda, diff_subln, w_o_diff, w_o_sb, w_out,
           norm_ffn, w_ffn_in, w_ffn_out, norm_final):
    b, s, d = x.shape
    cos_t, sin_t = _rope_tables(s)
    x2d = x.reshape(b * s, d)
    g_final = norm_final.reshape(1, d)
    w_in, w_o_diff, w_o_sb, w_out, w_ffn_in, w_ffn_out = (
        w.astype(BF16) for w in (w_in, w_o_diff, w_o_sb, w_out, w_ffn_in, w_ffn_out))
    for layer in range(DEPTH):
        lam_init = 0.8 - 0.6 * math.exp(-0.3 * layer)
        proj2d = _in_proj(x2d, norm_attn[layer].reshape(1, d), w_in, layer,
                          b_gate[layer].reshape(1, 2 * d), cos_t, sin_t, s)
        proj = proj2d.reshape(b, s, IN_COLS)
        o_diff, o_sb = _attention(proj, diff_lambda[layer],
                                  diff_subln[layer].reshape(LANES, 1), lam_init)
        x2d = _merge_ffn(
            x2d, o_diff.reshape(b * s, DIFF_WIDTH), o_sb.reshape(b * s, SB_WIDTH), proj2d,
            layer, w_o_diff, w_o_sb, w_out, norm_ffn[layer].reshape(1, d), w_ffn_in,
            w_ffn_out, g_final, layer == DEPTH - 1)
    return x2d.reshape(b, s, d)
```
